```python
import math
import jax, jax.numpy as jnp
from jax import lax
import numpy as np

D_MODEL = 2048
BATCH = 8
SEQ = 2048
DEPTH = 2

HEAD_DIM = 64
GROUP_WIDTH = D_MODEL // 4
D_MIX = 4 * GROUP_WIDTH
H_A = GROUP_WIDTH // HEAD_DIM
G_A = 2
CMP_BLK = 32
CMP_STRIDE = 16
CMP_HIDDEN = 128
SLC_BLK = 64
N_SEL = 8
WIN_A = 512
H_B = GROUP_WIDTH // HEAD_DIM
G_B = 2
WIN_B = 128
CONV_CH = GROUP_WIDTH
CONV_W = 3
H_D = GROUP_WIDTH // HEAD_DIM
DILATED = ((128, 1), (512, 4), (2048, 16))
NUM_BUCKETS = 32
MAX_DISTANCE = 2048
N_BIAS_HEADS = H_A + H_B + H_D
BLOCK = 128
RMS_EPS = 1e-6
NEG = -1e30
FORCE = 1e4

SEG_SIZES = (
    H_A * HEAD_DIM,
    G_A * HEAD_DIM, G_A * HEAD_DIM,
    G_A * HEAD_DIM, G_A * HEAD_DIM,
    G_A * HEAD_DIM, G_A * HEAD_DIM,
    3 * H_A,
    GROUP_WIDTH,
    H_B * HEAD_DIM, G_B * HEAD_DIM, G_B * HEAD_DIM, GROUP_WIDTH,
    CONV_CH, CONV_CH, CONV_CH, GROUP_WIDTH,
    H_D * HEAD_DIM, H_D * HEAD_DIM, H_D * HEAD_DIM, GROUP_WIDTH,
)
D_IN = sum(SEG_SIZES)

kernel_name = "hybrid_nsa_swa_conv_dilated_block"


def rms_norm(x, w):
    xf = x.astype(jnp.float32)
    y = xf * lax.rsqrt(jnp.mean(xf * xf, axis=-1, keepdims=True) + RMS_EPS)
    return (y * w.astype(jnp.float32)).astype(x.dtype)


def t5_bucket(dist):
    exact = NUM_BUCKETS // 2
    d = jnp.maximum(dist, 0)
    large = exact + (jnp.log(jnp.maximum(d, exact).astype(jnp.float32) / exact)
                     / math.log(MAX_DISTANCE / exact) * (NUM_BUCKETS - exact)).astype(jnp.int32)
    return jnp.where(d < exact, d, jnp.minimum(large, NUM_BUCKETS - 1))


def masked_softmax(s, mask):
    s = jnp.where(mask, s, NEG)
    e = jnp.where(mask, jnp.exp(s - jnp.max(s, axis=-1, keepdims=True)), 0.0)
    return e / jnp.maximum(jnp.sum(e, axis=-1, keepdims=True), 1e-30)


def to_heads(t, n):
    b, s, _ = t.shape
    return t.reshape(b, s, n, HEAD_DIM).transpose(0, 2, 1, 3)


def from_heads(t):
    b, n, s, hd = t.shape
    return t.transpose(0, 2, 1, 3).reshape(b, s, n * hd)


def banded_attention(q, k, v, max_dist, bias_tab, dist_scale=1, sink=None):
    b, h, s, hd = q.shape
    hkv = k.shape[1]
    r = h // hkv
    nprev = -(-max_dist // BLOCK)
    nb = -(-s // BLOCK)
    sp = nb * BLOCK
    pad = sp - s
    qp = jnp.pad(q, ((0, 0), (0, 0), (0, pad), (0, 0)))
    kp = jnp.pad(k, ((0, 0), (0, 0), (nprev * BLOCK, pad), (0, 0))).reshape(b, hkv, nb + nprev, BLOCK, hd)
    vp = jnp.pad(v, ((0, 0), (0, 0), (nprev * BLOCK, pad), (0, 0))).reshape(b, hkv, nb + nprev, BLOCK, hd)
    kw = jnp.concatenate([kp[:, :, j:j + nb] for j in range(nprev + 1)], axis=3)
    vw = jnp.concatenate([vp[:, :, j:j + nb] for j in range(nprev + 1)], axis=3)
    wlen = (nprev + 1) * BLOCK
    qb = qp.reshape(b, hkv, r, nb, BLOCK, hd)
    sc = jnp.einsum('bgrcqd,bgckd->bgrcqk', qb, kw).astype(jnp.float32) * (hd ** -0.5)
    qi = jnp.arange(BLOCK)[:, None]
    kj = jnp.arange(wlen)[None, :]
    dist = qi - kj + nprev * BLOCK
    kpos = jnp.arange(nb)[:, None, None] * BLOCK + kj[None] - nprev * BLOCK
    mask = (dist >= 0)[None] & (dist <= max_dist)[None] & (kpos >= 0)
    bias = bias_tab[t5_bucket(dist * dist_scale)].astype(jnp.float32)
    bias = bias.transpose(2, 0, 1).reshape(hkv, r, BLOCK, wlen)[None, :, :, None]
    sc = jnp.where(mask, sc + bias, NEG)
    lse = jax.nn.logsumexp(sc, axis=-1)
    if sink is not None:
        lse = jnp.logaddexp(lse, sink.astype(jnp.float32).reshape(hkv, r)[None, :, :, None, None])
    p = jnp.exp(sc - lse[..., None])
    o = jnp.einsum('bgrcqk,bgckd->bgrcqd', p, vw).reshape(b, h, sp, hd)[:, :, :s]
    return o, lse.reshape(b, h, sp)[:, :, :s]


def nsa_mixer(q, kc, vc, ks, vs, kw, vw, gates, cmp_pos, cmp_w1, cmp_w2, bias_tab):
    b, h, s, hd = q.shape
    g = kc.shape[1]
    r = h // g
    scale = hd ** -0.5
    t_pos = jnp.arange(s)
    qg = q.reshape(b, g, r, s, hd)

    n_cmp = (s - CMP_BLK) // CMP_STRIDE + 1
    cmp_start = jnp.arange(n_cmp) * CMP_STRIDE
    tok = cmp_start[:, None] + jnp.arange(CMP_BLK)[None]

    def compress(t, pos, w1, w2):
        blocks = t[:, :, tok] + pos
        flat = blocks.reshape(b, g, n_cmp, CMP_BLK * hd)
        return jax.nn.silu(flat @ w1) @ w2

    kcmp = compress(kc, cmp_pos[0], cmp_w1[0], cmp_w2[0])
    vcmp = compress(vc, cmp_pos[1], cmp_w1[1], cmp_w2[1])
    dist_c = t_pos[:, None] - (cmp_start + CMP_BLK - 1)[None]
    bias_c = bias_tab[t5_bucket(dist_c)].astype(jnp.float32).transpose(2, 0, 1).reshape(g, r, s, n_cmp)
    sc = jnp.einsum('bgrsd,bgcd->bgrsc', qg, kcmp).astype(jnp.float32) * scale + bias_c
    p_cmp = masked_softmax(sc, dist_c >= 0)
    o_cmp = jnp.einsum('bgrsc,bgcd->bgrsd', p_cmp, vcmp).reshape(b, h, s, hd)

    n_slc = s // SLC_BLK
    n_sel = min(N_SEL, n_slc)
    slc_start = jnp.arange(n_slc) * SLC_BLK
    overlap = ((cmp_start[:, None] < slc_start[None] + SLC_BLK)
               & (cmp_start[:, None] + CMP_BLK > slc_start[None])).astype(jnp.float32)
    imp = jnp.einsum('bgrsc,cn->bgsn', p_cmp, overlap)
    cur = (t_pos // SLC_BLK)[:, None]
    blk = jnp.arange(n_slc)[None]
    forced = (blk == 0) | (blk == cur) | (blk == cur - 1)
    imp = jnp.where(blk > cur, NEG, jnp.where(forced, FORCE, imp))
    _, sel = lax.top_k(imp, n_sel)
    ksb = ks.reshape(b, g, n_slc, SLC_BLK, hd)
    vsb = vs.reshape(b, g, n_slc, SLC_BLK, hd)
    tab_g = bias_tab.reshape(NUM_BUCKETS, g, r).transpose(1, 0, 2)
    gather = jax.vmap(jax.vmap(lambda blocks, ids: blocks[ids]))
    n_tok = n_sel * SLC_BLK

    def sel_chunk(args):
        qc, idc, tq = args
        nq = tq.shape[0]
        kg = gather(ksb, idc).reshape(b, g, nq, n_tok, hd)
        vg = gather(vsb, idc).reshape(b, g, nq, n_tok, hd)
        kpos = (idc[..., None] * SLC_BLK + jnp.arange(SLC_BLK)).reshape(b, g, nq, n_tok)
        dist = tq[:, None] - kpos
        bias = jax.vmap(lambda tb, bk: tb[bk], in_axes=(0, 1), out_axes=1)(tab_g, t5_bucket(dist))
        bias = jnp.moveaxis(bias, -1, 2).astype(jnp.float32)
        sc_s = jnp.einsum('bgrqd,bgqtd->bgrqt', qc, kg).astype(jnp.float32) * scale + bias
        p = masked_softmax(sc_s, (dist >= 0)[:, :, None])
        return jnp.einsum('bgrqt,bgqtd->bgrqd', p, vg)

    nchunk = s // BLOCK
    q_ch = jnp.moveaxis(qg.reshape(b, g, r, nchunk, BLOCK, hd), 3, 0)
    id_ch = jnp.moveaxis(sel.reshape(b, g, nchunk, BLOCK, n_sel), 2, 0)
    t_ch = t_pos.reshape(nchunk, BLOCK)
    o_slc = lax.map(sel_chunk, (q_ch, id_ch, t_ch))
    o_slc = jnp.moveaxis(o_slc, 0, 3).reshape(b, h, s, hd)

    o_win, _ = banded_attention(q, kw, vw, WIN_A - 1, bias_tab)

    gt = jax.nn.sigmoid(gates.astype(jnp.float32)).reshape(b, s, 3, h).transpose(2, 0, 3, 1)[..., None]
    return gt[0] * o_cmp + gt[1] * o_slc + gt[2] * o_win


def short_conv_mixer(bg, cg, hx, conv_w):
    u = cg * hx
    y = lax.conv_general_dilated(u, conv_w[:, None, :].astype(u.dtype), window_strides=(1,),
                                 padding=[(CONV_W - 1, 0)], dimension_numbers=('NWC', 'WIO', 'NWC'),
                                 feature_group_count=u.shape[-1])
    return bg * y


def dilated_mixer(q, k, v, bias_tab):
    b, h, s, hd = q.shape
    outs, lses = [], []
    for window, dil in DILATED:
        def split(t):
            return t.reshape(b, h, s // dil, dil, hd).transpose(0, 3, 1, 2, 4).reshape(b * dil, h, s // dil, hd)
        o, lse = banded_attention(split(q), split(k), split(v), window // dil, bias_tab, dist_scale=dil)
        outs.append(o.reshape(b, dil, h, s // dil, hd).transpose(0, 2, 3, 1, 4).reshape(b, h, s, hd))
        lses.append(lse.reshape(b, dil, h, s // dil).transpose(0, 2, 3, 1).reshape(b, h, s))
    wts = jax.nn.softmax(jnp.stack(lses), axis=0)
    return jnp.einsum('pbhs,pbhsd->bhsd', wts, jnp.stack(outs))


def hybrid_layer(x, norm_w, w_in, w_out, conv_w, sinks, cmp_pos, cmp_w1, cmp_w2, rel_bias):
    hn = rms_norm(x, norm_w)
    proj = hn @ w_in
    offs = np.cumsum(SEG_SIZES)[:-1].tolist()
    (aq, akc, avc, aks, avs, akw, avw, agates, agate,
     bq, bk, bv, bgate, cb, cc, ch, cgate, dq, dk, dv, dgate) = jnp.split(proj, offs, axis=-1)

    o_a = nsa_mixer(to_heads(aq, H_A), to_heads(akc, G_A), to_heads(avc, G_A), to_heads(aks, G_A),
                    to_heads(avs, G_A), to_heads(akw, G_A), to_heads(avw, G_A), agates,
                    cmp_pos, cmp_w1, cmp_w2, rel_bias[:, :H_A])
    o_b, _ = banded_attention(to_heads(bq, H_B), to_heads(bk, G_B), to_heads(bv, G_B), WIN_B - 1,
                              rel_bias[:, H_A:H_A + H_B], sink=sinks)
    o_c = short_conv_mixer(cb, cc, ch, conv_w)
    o_d = dilated_mixer(to_heads(dq, H_D), to_heads(dk, H_D), to_heads(dv, H_D), rel_bias[:, H_A + H_B:])

    mix = jnp.concatenate([
        from_heads(o_a).astype(x.dtype) * jax.nn.silu(agate),
        from_heads(o_b).astype(x.dtype) * jax.nn.silu(bgate),
        o_c.astype(x.dtype) * jax.nn.silu(cgate),
        from_heads(o_d).astype(x.dtype) * jax.nn.silu(dgate),
    ], axis=-1)
    return x + (mix @ w_out).astype(x.dtype)


def setup_inputs(seed: int = 0) -> dict:
    key = jax.random.key(seed)
    ks = jax.random.split(key, 12)
    f32 = jnp.float32
    x = jax.random.normal(ks[0], (BATCH, SEQ, D_MODEL), f32)
    norm_w = 1.0 + 0.02 * jax.random.normal(ks[1], (DEPTH, D_MODEL), f32)
    w_in = jax.random.normal(ks[2], (DEPTH, D_MODEL, D_IN), f32) * D_MODEL ** -0.5
    w_out = jax.random.normal(ks[3], (DEPTH, D_MIX, D_MODEL), f32) * D_MIX ** -0.5
    conv_w = jax.random.normal(ks[4], (DEPTH, CONV_W, CONV_CH), f32) * CONV_W ** -0.5
    sinks = 0.5 * jax.random.normal(ks[5], (DEPTH, H_B), f32)
    cmp_pos = 0.1 * jax.random.normal(ks[6], (DEPTH, 2, CMP_BLK, HEAD_DIM), f32)
    cmp_w1 = jax.random.normal(ks[7], (DEPTH, 2, CMP_BLK * HEAD_DIM, CMP_HIDDEN), f32) * (CMP_BLK * HEAD_DIM) ** -0.5
    cmp_w2 = jax.random.normal(ks[8], (DEPTH, 2, CMP_HIDDEN, HEAD_DIM), f32) * CMP_HIDDEN ** -0.5
    rel_bias = 0.5 * jax.random.normal(ks[9], (NUM_BUCKETS, N_BIAS_HEADS), f32)
    final_norm_w = 1.0 + 0.02 * jax.random.normal(ks[10], (D_MODEL,), f32)
    return {"x": x, "norm_w": norm_w, "w_in": w_in, "w_out": w_out, "conv_w": conv_w,
            "sinks": sinks, "cmp_pos": cmp_pos, "cmp_w1": cmp_w1, "cmp_w2": cmp_w2,
            "rel_bias": rel_bias, "final_norm_w": final_norm_w}


def reference(x, norm_w, w_in, w_out, conv_w, sinks, cmp_pos, cmp_w1, cmp_w2, rel_bias, final_norm_w):
    for layer in range(DEPTH):
        x = hybrid_layer(x, norm_w[layer], w_in[layer], w_out[layer], conv_w[layer], sinks[layer],
                         cmp_pos[layer], cmp_w1[layer], cmp_w2[layer], rel_bias)
    return rms_norm(x, final_norm_w)
```

```python
import functools
import math

import numpy as np
import jax
import jax.numpy as jnp
from jax import lax
from jax.experimental import pallas as pl
from jax.experimental.pallas import tpu as pltpu

F32 = jnp.float32
BF16 = jnp.bfloat16

D_MODEL = 2048
HEAD_DIM = 64
GROUP_WIDTH = 512
N_HEADS = 8
N_KV = 2
CMP_BLK = 32
CMP_STRIDE = 16
CMP_HIDDEN = 128
SLC_BLK = 64
N_SEL = 8
WIN_A = 512
WIN_B = 128
DIL_MAXDIST = 128
NUM_BUCKETS = 32
MAX_DISTANCE = 2048
RMS_EPS = 1e-6
NEG = -1e30
FORCE = 1e4
D_IN = 7192

LANES = 128
TILE = 128
VMEM_LIMIT = 56 * 1024 * 1024

CB_AQ = 0
CB_ACMP = 4
CB_AKS, CB_AVS, CB_AKW, CB_AVW = 6, 7, 8, 9
CB_AGATE = 10
CB_BQ, CB_BK, CB_BV, CB_BGATE = 14, 18, 19, 20
CB_CB, CB_CC, CB_CH, CB_CGATE = 24, 28, 32, 36
CB_DQ, CB_DK, CB_DV, CB_DGATE = 40, 44, 48, 52
CB_AGATES = 56
N_CB = 57
CB_PER_STEP = 19


def _cparams(*sem):
    return pltpu.CompilerParams(dimension_semantics=sem, vmem_limit_bytes=VMEM_LIMIT)


def _lane_lo():
    return lax.broadcasted_iota(jnp.int32, (1, LANES), 1) < HEAD_DIM


def _dot_nt(a, b):
    return lax.dot_general(a, b, (((1,), (1,)), ((), ())), preferred_element_type=F32)


def _dot(a, b):
    return jnp.dot(a, b, preferred_element_type=F32)


def _silu(x):
    return x * jax.nn.sigmoid(x)


def _inproj_kernel(x_ref, nw_ref, w_ref, o_ref):
    x = x_ref[...]
    y = x * lax.rsqrt(jnp.mean(x * x, axis=-1, keepdims=True) + RMS_EPS)
    xn = (y * nw_ref[...]).astype(BF16)
    acc = _dot(xn, w_ref[...])
    for k in range(CB_PER_STEP):
        o_ref[k] = acc[:, k * LANES:(k + 1) * LANES]


def _inproj(x2, norm_w, w_perm, tm=512):
    m = x2.shape[0]
    n_steps = N_CB // CB_PER_STEP
    return pl.pallas_call(
        _inproj_kernel,
        grid=(n_steps, m // tm),
        in_specs=[
            pl.BlockSpec((tm, D_MODEL), lambda j, i: (i, 0)),
            pl.BlockSpec((1, D_MODEL), lambda j, i: (0, 0)),
            pl.BlockSpec((D_MODEL, CB_PER_STEP * LANES), lambda j, i: (0, j)),
        ],
        out_specs=pl.BlockSpec((CB_PER_STEP, tm, LANES), lambda j, i: (j, i, 0)),
        out_shape=jax.ShapeDtypeStruct((N_CB, m, LANES), F32),
        compiler_params=_cparams("arbitrary", "arbitrary"),
        name="inproj",
    )(x2, norm_w.reshape(1, D_MODEL), w_perm)


def _outproj_kernel(a_ref, b_ref, c_ref, d_ref, w_ref, x_ref, fw_ref, o_ref, *, final):
    mix = jnp.concatenate([r[k] for r in (a_ref, b_ref, c_ref, d_ref) for k in range(4)], axis=1)
    y = x_ref[...] + _dot(mix.astype(BF16), w_ref[...])
    if final:
        y = y * lax.rsqrt(jnp.mean(y * y, axis=-1, keepdims=True) + RMS_EPS) * fw_ref[...]
    o_ref[...] = y


def _outproj(mix_a, mix_b, mix_c, mix_d, w_out_bf, x2, final_w, final, tm=512):
    m = x2.shape[0]
    mspec = pl.BlockSpec((4, tm, LANES), lambda i: (0, i, 0))
    return pl.pallas_call(
        functools.partial(_outproj_kernel, final=final),
        grid=(m // tm,),
        in_specs=[mspec, mspec, mspec, mspec,
                  pl.BlockSpec((D_MODEL, D_MODEL), lambda i: (0, 0)),
                  pl.BlockSpec((tm, D_MODEL), lambda i: (i, 0)),
                  pl.BlockSpec((1, D_MODEL), lambda i: (0, 0))],
        out_specs=pl.BlockSpec((tm, D_MODEL), lambda i: (i, 0)),
        out_shape=jax.ShapeDtypeStruct((m, D_MODEL), F32),
        compiler_params=_cparams("arbitrary"),
        name="outproj",
    )(mix_a, mix_b, mix_c, mix_d, w_out_bf, x2, final_w.reshape(1, D_MODEL))


def _flash_pair(q2, i, nd, kp, vp, bias_fn, extra_fn=None):
    lo = _lane_lo()

    def step(d, carry):
        m0, l0, m1, l1, acc = carry
        j = i - d
        off = pl.multiple_of(j * TILE, TILE)
        ext = extra_fn(j) if extra_fn is not None else None
        res = []
        pv = None
        for hh, (m_old, l_old) in enumerate(((m0, l0), (m1, l1))):
            s = _dot_nt(q2, kp[hh][pl.ds(off, TILE), :]) + bias_fn(hh, d)
            if ext is not None:
                s = s + ext
            m_new = jnp.maximum(m_old, jnp.max(s, axis=-1, keepdims=True))
            alpha = jnp.exp(m_old - m_new)
            p = jnp.exp(s - m_new)
            l_new = alpha * l_old + jnp.sum(p, axis=-1, keepdims=True)
            o = _dot(p.astype(BF16), vp[hh][pl.ds(off, TILE), :])
            pv = o if pv is None else pv + o
            res.append((m_new, l_new, alpha))
        acc = acc * jnp.where(lo, res[0][2], res[1][2]) + pv
        return res[0][0], res[0][1], res[1][0], res[1][1], acc

    init = (jnp.full((TILE, 1), NEG, F32), jnp.zeros((TILE, 1), F32),
            jnp.full((TILE, 1), NEG, F32), jnp.zeros((TILE, 1), F32),
            jnp.zeros((TILE, LANES), F32))
    return lax.fori_loop(0, nd, step, init)


def _store_group_kv(g, src_ref, dst0, dst1):
    lo = _lane_lo()
    x = src_ref[...]
    own = jnp.where(lo == (g == 0), x, 0.0)
    both = own + pltpu.roll(own, HEAD_DIM, axis=1)
    dst0[...] = jnp.where(lo, both, 0.0).astype(BF16)
    dst1[...] = jnp.where(lo, 0.0, both).astype(BF16)


def _store_pair_kv(src, dst0, dst1):
    lo = _lane_lo()
    dst0[...] = jnp.where(lo, src, 0.0).astype(BF16)
    dst1[...] = jnp.where(lo, 0.0, src).astype(BF16)


def _swa_kernel(sink_ref, q_ref, k_ref, v_ref, gate_ref, bias_ref, o_ref, kp0, kp1, vp0, vp1, *, seq):
    g = pl.program_id(1)
    lo = _lane_lo()
    _store_group_kv(g, k_ref, kp0, kp1)
    _store_group_kv(g, v_ref, vp0, vp1)

    def qtile(i, _):
        rows = pl.ds(pl.multiple_of(i * TILE, TILE), TILE)
        for cb in range(2):
            q2 = (q_ref[cb, rows, :] * (HEAD_DIM ** -0.5)).astype(BF16)
            m0, l0, m1, l1, acc = _flash_pair(
                q2, i, jnp.minimum(i + 1, 2), (kp0, kp1), (vp0, vp1),
                lambda hh, d, cb=cb: bias_ref[2 * cb + hh, d])
            inv = []
            for hh, (m, l) in enumerate(((m0, l0), (m1, l1))):
                sink = sink_ref[g * 4 + 2 * cb + hh]
                mt = jnp.maximum(m, sink)
                scale = jnp.exp(m - mt)
                inv.append(scale / (l * scale + jnp.exp(sink - mt)))
            o = acc * jnp.where(lo, inv[0], inv[1])
            o_ref[cb, rows, :] = o * _silu(gate_ref[cb, rows, :])
        return 0

    lax.fori_loop(0, seq // TILE, qtile, 0)


def _swa(proj4, sinks, t_b):
    _, bsz, seq, _ = proj4.shape
    return pl.pallas_call(
        functools.partial(_swa_kernel, seq=seq),
        grid=(bsz, N_KV),
        in_specs=[
            pl.BlockSpec(memory_space=pltpu.SMEM),
            pl.BlockSpec((2, None, seq, LANES), lambda b, g: (CB_BQ // 2 + g, b, 0, 0)),
            pl.BlockSpec((None, None, seq, LANES), lambda b, g: (CB_BK, b, 0, 0)),
            pl.BlockSpec((None, None, seq, LANES), lambda b, g: (CB_BV, b, 0, 0)),
            pl.BlockSpec((2, None, seq, LANES), lambda b, g: (CB_BGATE // 2 + g, b, 0, 0)),
            pl.BlockSpec((4, 2, TILE, TILE), lambda b, g: (g, 0, 0, 0)),
        ],
        out_specs=pl.BlockSpec((2, None, seq, LANES), lambda b, g: (g, b, 0, 0)),
        out_shape=jax.ShapeDtypeStruct((4, bsz, seq, LANES), F32),
        scratch_shapes=[pltpu.VMEM((seq, LANES), BF16)] * 4,
        compiler_params=_cparams("arbitrary", "arbitrary"),
        name="swa",
    )(sinks, proj4, proj4, proj4, proj4, t_b)


def _conv_kernel(cw_ref, b_ref, c_ref, h_ref, gate_ref, o_ref, pad_ref, *, seq):
    pad_ref[0:8, :] = jnp.zeros((8, LANES), F32)
    pad_ref[8:8 + seq, :] = c_ref[...] * h_ref[...]
    w = cw_ref[...]
    y = (w[0:1, :] * pad_ref[6:6 + seq, :] + w[1:2, :] * pad_ref[7:7 + seq, :]
         + w[2:3, :] * pad_ref[8:8 + seq, :])
    o_ref[...] = b_ref[...] * y * _silu(gate_ref[...])


def _conv(proj4, conv_w):
    _, bsz, seq, _ = proj4.shape
    cw = jnp.pad(conv_w, ((0, 5), (0, 0)))

    def spec(base):
        return pl.BlockSpec((None, None, seq, LANES), lambda b, c: (base + c, b, 0, 0))

    return pl.pallas_call(
        functools.partial(_conv_kernel, seq=seq),
        grid=(bsz, 4),
        in_specs=[pl.BlockSpec((8, LANES), lambda b, c: (0, c)),
                  spec(CB_CB), spec(CB_CC), spec(CB_CH), spec(CB_CGATE)],
        out_specs=pl.BlockSpec((None, None, seq, LANES), lambda b, c: (c, b, 0, 0)),
        out_shape=jax.ShapeDtypeStruct((4, bsz, seq, LANES), F32),
        scratch_shapes=[pltpu.VMEM((seq + 8, LANES), F32)],
        compiler_params=_cparams("arbitrary", "arbitrary"),
        name="conv",
    )(cw, proj4, proj4, proj4, proj4)


def _dil1_kernel(q_ref, k_ref, v_ref, bias_ref, o_ref, lse_ref, kp0, kp1, vp0, vp1, *, seq):
    lo = _lane_lo()
    _store_pair_kv(k_ref[...], kp0, kp1)
    _store_pair_kv(v_ref[...], vp0, vp1)

    def qtile(i, _):
        rows = pl.ds(pl.multiple_of(i * TILE, TILE), TILE)
        q2 = (q_ref[rows, :] * (HEAD_DIM ** -0.5)).astype(BF16)
        m0, l0, m1, l1, acc = _flash_pair(q2, i, jnp.minimum(i + 1, 2), (kp0, kp1), (vp0, vp1),
                                          lambda hh, d: bias_ref[hh, d])
        o_ref[rows, :] = acc / jnp.where(lo, l0, l1)
        lse_ref[rows, :] = jnp.where(lo, m0 + jnp.log(l0), m1 + jnp.log(l1))
        return 0

    lax.fori_loop(0, seq // TILE, qtile, 0)


def _dil1(proj4, t_d1):
    _, bsz, seq, _ = proj4.shape

    def spec(base):
        return pl.BlockSpec((None, None, seq, LANES), lambda b, h: (base + h, b, 0, 0))

    ospec = pl.BlockSpec((None, None, seq, LANES), lambda b, h: (h, b, 0, 0))
    oshape = jax.ShapeDtypeStruct((4, bsz, seq, LANES), F32)
    return pl.pallas_call(
        functools.partial(_dil1_kernel, seq=seq),
        grid=(bsz, 4),
        in_specs=[spec(CB_DQ), spec(CB_DK), spec(CB_DV),
                  pl.BlockSpec((None, 2, 2, TILE, TILE), lambda b, h: (h, 0, 0, 0, 0))],
        out_specs=(ospec, ospec),
        out_shape=(oshape, oshape),
        scratch_shapes=[pltpu.VMEM((seq, LANES), BF16)] * 4,
        compiler_params=_cparams("arbitrary", "arbitrary"),
        name="dil1",
    )(proj4, proj4, proj4, t_d1)


def _dil_kernel(q_ref, k_ref, v_ref, o1_ref, lse1_ref, gate_ref, t4_ref, t16_ref, o_ref,
                qs, kp0, kp1, vp0, vp1, o4s, l4s, o16s, l16s):
    lo = _lane_lo()
    for p in range(16):
        cols = slice(p * LANES, (p + 1) * LANES)
        qs[p] = (q_ref[:, cols] * (HEAD_DIM ** -0.5)).astype(BF16)
        _store_pair_kv(k_ref[:, cols], kp0.at[p], kp1.at[p])
        _store_pair_kv(v_ref[:, cols], vp0.at[p], vp1.at[p])
    kp, vp = (kp0, kp1), (vp0, vp1)

    def phase(p, _):
        q2 = qs[p]
        r = p % 4
        c = p // 4
        outs, lses = [], []
        for hh in range(2):
            s = _dot_nt(q2, kp[hh][p]) + t16_ref[hh]
            m = jnp.max(s, axis=-1, keepdims=True)
            e = jnp.exp(s - m)
            l = jnp.sum(e, axis=-1, keepdims=True)
            outs.append(_dot(e.astype(BF16), vp[hh][p]) / l)
            lses.append(m + jnp.log(l))
        o16s[p] = outs[0] + outs[1]
        l16s[p] = jnp.where(lo, lses[0], lses[1])

        def step(dd, carry):
            m0, l0, m1, l1, acc = carry
            c2 = (c + dd) % 4
            pk = 4 * c2 + r
            delta = c - c2 + 3
            res = []
            pv = None
            for hh, (m_old, l_old) in enumerate(((m0, l0), (m1, l1))):
                s = _dot_nt(q2, kp[hh][pk]) + t4_ref[hh, delta]
                m_new = jnp.maximum(m_old, jnp.max(s, axis=-1, keepdims=True))
                alpha = jnp.exp(m_old - m_new)
                e = jnp.exp(s - m_new)
                l_new = alpha * l_old + jnp.sum(e, axis=-1, keepdims=True)
                o = _dot(e.astype(BF16), vp[hh][pk])
                pv = o if pv is None else pv + o
                res.append((m_new, l_new, alpha))
            acc = acc * jnp.where(lo, res[0][2], res[1][2]) + pv
            return res[0][0], res[0][1], res[1][0], res[1][1], acc

        init = (jnp.full((TILE, 1), NEG, F32), jnp.zeros((TILE, 1), F32),
                jnp.full((TILE, 1), NEG, F32), jnp.zeros((TILE, 1), F32),
                jnp.zeros((TILE, LANES), F32))
        m0, l0, m1, l1, acc = lax.fori_loop(0, 4, step, init)
        o4s[p] = acc / jnp.where(lo, l0, l1)
        l4s[p] = jnp.where(lo, m0 + jnp.log(l0), m1 + jnp.log(l1))
        return 0

    lax.fori_loop(0, 16, phase, 0)

    for p in range(16):
        cols = slice(p * LANES, (p + 1) * LANES)
        la, lb, lc = lse1_ref[:, cols], l4s[p], l16s[p]
        mx = jnp.maximum(jnp.maximum(la, lb), lc)
        ea, eb, ec = jnp.exp(la - mx), jnp.exp(lb - mx), jnp.exp(lc - mx)
        o = (ea * o1_ref[:, cols] + eb * o4s[p] + ec * o16s[p]) / (ea + eb + ec)
        o_ref[:, cols] = o * _silu(gate_ref[:, cols])


def _dil(proj4, o1, lse1, t_d4, t_d16):
    _, bsz, seq, _ = proj4.shape
    na = seq // 16
    wide = 16 * LANES
    projv = proj4.reshape(N_CB, bsz, na, wide)
    o1v = o1.reshape(4, bsz, na, wide)
    lse1v = lse1.reshape(4, bsz, na, wide)

    def spec(base):
        return pl.BlockSpec((None, None, na, wide), lambda b, h: (base + h, b, 0, 0))

    out = pl.pallas_call(
        _dil_kernel,
        grid=(bsz, 4),
        in_specs=[spec(CB_DQ), spec(CB_DK), spec(CB_DV), spec(0), spec(0), spec(CB_DGATE),
                  pl.BlockSpec((None, 2, 7, TILE, TILE), lambda b, h: (h, 0, 0, 0, 0)),
                  pl.BlockSpec((None, 2, TILE, TILE), lambda b, h: (h, 0, 0, 0))],
        out_specs=spec(0),
        out_shape=jax.ShapeDtypeStruct((4, bsz, na, wide), F32),
        scratch_shapes=[pltpu.VMEM((16, na, LANES), BF16)] * 5 + [pltpu.VMEM((16, na, LANES), F32)] * 4,
        compiler_params=_cparams("arbitrary", "arbitrary"),
        name="dil",
    )(projv, projv, projv, o1v, lse1v, projv, t_d4, t_d16)
    return out.reshape(4, bsz, seq, LANES)


def _compress_kernel(r_ref, pos_ref, wlo_ref, whi_ref, w2_ref, o_ref):
    r = r_ref[...]
    h_lo = _dot((r + pos_ref[0:1, :]).astype(BF16), wlo_ref[...])
    h_hi = _dot((r + pos_ref[1:2, :]).astype(BF16), whi_ref[...])
    h = h_lo + pltpu.roll(h_hi, h_hi.shape[0] - 1, axis=0)
    o_ref[...] = _dot(_silu(h).astype(BF16), w2_ref[...])


def _compress(proj4, pos2, w_lo, w_hi, w2):
    _, bsz, seq, _ = proj4.shape
    na = seq // 16
    wide = 16 * LANES
    projv = proj4.reshape(N_CB, bsz, na, wide)
    full = lambda shape: pl.BlockSpec(shape, lambda b, g: (0,) * len(shape))
    return pl.pallas_call(
        _compress_kernel,
        grid=(bsz, N_KV),
        in_specs=[pl.BlockSpec((None, None, na, wide), lambda b, g: (CB_ACMP + g, b, 0, 0)),
                  full((2, wide)), full((wide, 2 * CMP_HIDDEN)), full((wide, 2 * CMP_HIDDEN)),
                  full((2 * CMP_HIDDEN, LANES))],
        out_specs=pl.BlockSpec((None, None, na, LANES), lambda b, g: (b, g, 0, 0)),
        out_shape=jax.ShapeDtypeStruct((bsz, N_KV, na, LANES), F32),
        compiler_params=_cparams("arbitrary", "arbitrary"),
        name="compress",
    )(projv, pos2, w_lo, w_hi, w2)


def _cmp_kernel(q_ref, kv_ref, bias_ref, ovt_ref, o_ref, sel_ref, *, seq):
    lo = _lane_lo()
    kv = kv_ref[...]
    k_lo = jnp.where(lo, kv, 0.0)
    v_hi = jnp.where(lo, 0.0, kv)
    kp = (k_lo.astype(BF16), pltpu.roll(k_lo, HEAD_DIM, axis=1).astype(BF16))
    vp = (pltpu.roll(v_hi, HEAD_DIM, axis=1).astype(BF16), v_hi.astype(BF16))
    ovt = ovt_ref[...]
    n_slc = ovt.shape[0]
    blk = lax.broadcasted_iota(jnp.int32, (n_slc, TILE), 0)
    eye = (lax.broadcasted_iota(jnp.int32, (TILE, TILE), 0)
           == lax.broadcasted_iota(jnp.int32, (TILE, TILE), 1)).astype(BF16)

    def qtile(i, _):
        rows = pl.ds(pl.multiple_of(i * TILE, TILE), TILE)
        psum = jnp.zeros((TILE, LANES), F32)
        for cb in range(2):
            q2 = (q_ref[cb, rows, :] * (HEAD_DIM ** -0.5)).astype(BF16)
            o2 = jnp.zeros((TILE, LANES), F32)
            for hh in range(2):
                s = _dot_nt(q2, kp[hh]) + bias_ref[2 * cb + hh, rows, :]
                m = jnp.maximum(jnp.max(s, axis=-1, keepdims=True), -1e20)
                e = jnp.exp(s - m)
                p = e / jnp.maximum(jnp.sum(e, axis=-1, keepdims=True), 1e-30)
                psum = psum + p
                o2 = o2 + _dot(p.astype(BF16), vp[hh])
            o_ref[cb, rows, :] = o2
        p1 = psum.astype(BF16)
        rem = psum - p1.astype(F32)
        p2 = rem.astype(BF16)
        p3 = (rem - p2.astype(F32)).astype(BF16)
        imp = _dot_nt(ovt, p1) + _dot_nt(ovt, p2) + _dot_nt(ovt, p3)
        t = i * TILE + lax.broadcasted_iota(jnp.int32, (n_slc, TILE), 1)
        cur = t // SLC_BLK
        forced = (blk == 0) | (blk == cur) | (blk == cur - 1)
        imp = jnp.where(blk > cur, NEG, jnp.where(forced, FORCE, imp))
        rank = jnp.zeros((n_slc, TILE), F32)
        for mth in range(n_slc):
            row = imp[mth:mth + 1, :]
            tie = (blk > mth).astype(F32)
            rank = rank + jnp.where(row > imp, 1.0, jnp.where(row == imp, tie, 0.0))
        sel_t = jnp.where(rank < N_SEL, 1.0, 0.0).astype(BF16)
        sel_t = jnp.concatenate([sel_t, jnp.zeros((TILE - n_slc, TILE), BF16)], axis=0)
        sel_ref[rows, :] = _dot_nt(eye, sel_t).astype(BF16)
        return 0

    lax.fori_loop(0, seq // TILE, qtile, 0)


def _cmp_select(proj4, kvcmp, t_cmp, ovt):
    _, bsz, seq, _ = proj4.shape
    return pl.pallas_call(
        functools.partial(_cmp_kernel, seq=seq),
        grid=(bsz, N_KV),
        in_specs=[pl.BlockSpec((2, None, seq, LANES), lambda b, g: (CB_AQ // 2 + g, b, 0, 0)),
                  pl.BlockSpec((None, None, TILE, LANES), lambda b, g: (b, g, 0, 0)),
                  pl.BlockSpec((4, seq, LANES), lambda b, g: (g, 0, 0)),
                  pl.BlockSpec(ovt.shape, lambda b, g: (0, 0))],
        out_specs=(pl.BlockSpec((2, None, seq, LANES), lambda b, g: (g, b, 0, 0)),
                   pl.BlockSpec((None, None, seq, LANES), lambda b, g: (b, g, 0, 0))),
        out_shape=(jax.ShapeDtypeStruct((4, bsz, seq, LANES), F32),
                   jax.ShapeDtypeStruct((bsz, N_KV, seq, LANES), BF16)),
        compiler_params=_cparams("arbitrary", "arbitrary"),
        name="cmp_select",
    )(proj4, kvcmp, t_cmp, ovt)


def _slcwin_kernel(q_ref, ks_ref, vs_ref, kw_ref, vw_ref, ocmp_ref, sel_ref, gates_ref, agate_ref,
                   tslc_ref, twin_ref, e_ref, x_ref, o_ref,
                   ksp0, ksp1, vsp0, vsp1, kwp0, kwp1, vwp0, vwp1, *, seq):
    g = pl.program_id(1)
    lo = _lane_lo()
    _store_group_kv(g, ks_ref, ksp0, ksp1)
    _store_group_kv(g, vs_ref, vsp0, vsp1)
    _store_group_kv(g, kw_ref, kwp0, kwp1)
    _store_group_kv(g, vw_ref, vwp0, vwp1)

    def qtile(i, _):
        rows = pl.ds(pl.multiple_of(i * TILE, TILE), TILE)
        sel = sel_ref[rows, :]
        sg = jax.nn.sigmoid(gates_ref[rows, :])
        sg1 = sg.astype(BF16)
        sg2 = (sg - sg1.astype(F32)).astype(BF16)

        def sel_mask(j):
            return (_dot(sel, e_ref[j]) - 1.0) * (-NEG)

        for cb in range(2):
            q2 = (q_ref[cb, rows, :] * (HEAD_DIM ** -0.5)).astype(BF16)
            _, l0, _, l1, acc = _flash_pair(q2, i, i + 1, (ksp0, ksp1), (vsp0, vsp1),
                                            lambda hh, d, cb=cb: tslc_ref[2 * cb + hh, d], sel_mask)
            o_slc = acc / jnp.where(lo, l0, l1)
            _, l0, _, l1, acc = _flash_pair(q2, i, jnp.minimum(i + 1, 5), (kwp0, kwp1), (vwp0, vwp1),
                                            lambda hh, d, cb=cb: twin_ref[2 * cb + hh, d])
            o_win = acc / jnp.where(lo, l0, l1)
            gts = [_dot(sg1, x_ref[br, cb]) + _dot(sg2, x_ref[br, cb]) for br in range(3)]
            o = gts[0] * ocmp_ref[cb, rows, :] + gts[1] * o_slc + gts[2] * o_win
            o_ref[cb, rows, :] = o * _silu(agate_ref[cb, rows, :])
        return 0

    lax.fori_loop(0, seq // TILE, qtile, 0)


def _slcwin(proj4, o_cmp, sel, t_slc, t_win, e_tab, x_tab):
    _, bsz, seq, _ = proj4.shape
    nt = seq // TILE

    def one(cb):
        return pl.BlockSpec((None, None, seq, LANES), lambda b, g: (cb, b, 0, 0))

    def two(base):
        return pl.BlockSpec((2, None, seq, LANES), lambda b, g: (base // 2 + g, b, 0, 0))

    return pl.pallas_call(
        functools.partial(_slcwin_kernel, seq=seq),
        grid=(bsz, N_KV),
        in_specs=[two(CB_AQ), one(CB_AKS), one(CB_AVS), one(CB_AKW), one(CB_AVW),
                  pl.BlockSpec((2, None, seq, LANES), lambda b, g: (g, b, 0, 0)),
                  pl.BlockSpec((None, None, seq, LANES), lambda b, g: (b, g, 0, 0)),
                  one(CB_AGATES), two(CB_AGATE),
                  pl.BlockSpec((4, nt, TILE, TILE), lambda b, g: (g, 0, 0, 0)),
                  pl.BlockSpec((4, 5, TILE, TILE), lambda b, g: (g, 0, 0, 0)),
                  pl.BlockSpec((nt, TILE, TILE), lambda b, g: (0, 0, 0)),
                  pl.BlockSpec((None, 3, 2, TILE, TILE), lambda b, g: (g, 0, 0, 0, 0))],
        out_specs=pl.BlockSpec((2, None, seq, LANES), lambda b, g: (g, b, 0, 0)),
        out_shape=jax.ShapeDtypeStruct((4, bsz, seq, LANES), F32),
        scratch_shapes=[pltpu.VMEM((seq, LANES), BF16)] * 8,
        compiler_params=_cparams("arbitrary", "arbitrary"),
        name="slcwin",
    )(proj4, proj4, proj4, proj4, proj4, o_cmp, sel, proj4, proj4, t_slc, t_win, e_tab, x_tab)


def _t5_bucket(d):
    exact = NUM_BUCKETS // 2
    large = exact + (jnp.log(jnp.maximum(d, exact).astype(F32) / exact)
                     / math.log(MAX_DISTANCE / exact) * (NUM_BUCKETS - exact)).astype(jnp.int32)
    return jnp.where(d < exact, d, jnp.minimum(large, NUM_BUCKETS - 1))


def _toeplitz(bias_d, dist, valid):
    idx = np.clip(dist, 0, bias_d.shape[1] - 1).astype(np.int32)
    return jnp.where(valid, bias_d[:, idx], NEG)


def _bias_tables(rel_bias, seq):
    bias_d = rel_bias[_t5_bucket(jnp.arange(seq, dtype=jnp.int32))].T
    ba, bb, bd = bias_d[:8], bias_d[8:16], bias_d[16:24]
    a = np.arange(TILE)[:, None]
    b = np.arange(TILE)[None, :]
    nt = seq // TILE
    dist = np.arange(nt)[:, None, None] * TILE + (a - b)[None]
    t_slc = _toeplitz(ba, dist, dist >= 0)
    t_win = _toeplitz(ba, dist[:5], (dist[:5] >= 0) & (dist[:5] <= WIN_A - 1))
    t_b = _toeplitz(bb, dist[:2], (dist[:2] >= 0) & (dist[:2] <= WIN_B - 1))
    t_d1 = _toeplitz(bd, dist[:2], (dist[:2] >= 0) & (dist[:2] <= DIL_MAXDIST))
    delta = np.arange(-3, 4)[:, None, None]
    du = 4 * (a - b)[None] + delta
    t_d4 = _toeplitz(bd, 4 * du, (du >= 0) & (du <= DIL_MAXDIST))
    t_d16 = _toeplitz(bd, 16 * (a - b), (a - b) >= 0)
    n_cmp = (seq - CMP_BLK) // CMP_STRIDE + 1
    t = np.arange(seq)[:, None]
    c = np.arange(TILE)[None, :]
    dist_c = t - (c * CMP_STRIDE + CMP_BLK - 1)
    t_cmp = _toeplitz(ba, dist_c, (dist_c >= 0) & (c < n_cmp))
    return dict(slc=t_slc, win=t_win, b=t_b, d1=t_d1.reshape(4, 2, 2, TILE, TILE),
                d4=t_d4.reshape(4, 2, 7, TILE, TILE), d16=t_d16.reshape(4, 2, TILE, TILE), cmp=t_cmp)


def _const_tables(seq):
    n_cmp = (seq - CMP_BLK) // CMP_STRIDE + 1
    n_slc = seq // SLC_BLK
    c0 = np.arange(TILE)[None, :] * CMP_STRIDE
    s0 = np.arange(n_slc)[:, None] * SLC_BLK
    ovt = ((c0 < s0 + SLC_BLK) & (c0 + CMP_BLK > s0) & (np.arange(TILE)[None, :] < n_cmp))
    nt = seq // TILE
    key = np.arange(nt)[:, None, None] * TILE + np.arange(TILE)[None, None, :]
    e_tab = (np.arange(TILE)[None, :, None] == key // SLC_BLK)
    row = np.arange(TILE)[None, None, None, :, None]
    lane = np.arange(TILE)[None, None, None, None, :]
    g = np.arange(N_KV)[:, None, None, None, None]
    br = np.arange(3)[None, :, None, None, None]
    cb = np.arange(2)[None, None, :, None, None]
    x_tab = (row == br * N_HEADS + 4 * g + 2 * cb + lane // HEAD_DIM)
    as_bf = lambda m: jnp.asarray(m.astype(np.float32), BF16)
    return as_bf(ovt), as_bf(e_tab), as_bf(x_tab)


def _permute_w_in(w):
    pad = jnp.zeros((w.shape[0], N_CB * LANES - D_IN), w.dtype)
    return jnp.concatenate([
        w[:, 0:512],
        w[:, 512:576], w[:, 640:704], w[:, 576:640], w[:, 704:768],
        w[:, 768:1280],
        w[:, 1304:D_IN],
        w[:, 1280:1304], pad], axis=1).astype(BF16)


def _compress_weights(cmp_pos, cmp_w1, cmp_w2):
    half = CMP_BLK // 2
    w1 = cmp_w1.reshape(2, CMP_BLK, HEAD_DIM, CMP_HIDDEN)
    z = jnp.zeros((half, HEAD_DIM, CMP_HIDDEN), F32)

    def stack(part):
        top = jnp.concatenate([w1[0, part], z], axis=-1)
        bot = jnp.concatenate([z, w1[1, part]], axis=-1)
        return jnp.concatenate([top, bot], axis=1).reshape(half * LANES, 2 * CMP_HIDDEN).astype(BF16)

    w_lo, w_hi = stack(slice(0, half)), stack(slice(half, CMP_BLK))
    z2 = jnp.zeros((CMP_HIDDEN, HEAD_DIM), F32)
    w2 = jnp.concatenate([jnp.concatenate([cmp_w2[0], z2], axis=1),
                          jnp.concatenate([z2, cmp_w2[1]], axis=1)], axis=0).astype(BF16)
    pos = jnp.concatenate([cmp_pos[0], cmp_pos[1]], axis=-1)
    pos2 = pos.reshape(2, half * LANES)
    return pos2, w_lo, w_hi, w2


def _layer(x2, bsz, seq, norm_w, w_in, w_out, conv_w, sinks, cmp_pos, cmp_w1, cmp_w2, final_w, final,
           tabs, consts):
    ovt, e_tab, x_tab = consts
    proj = _inproj(x2, norm_w, _permute_w_in(w_in))
    proj4 = proj.reshape(N_CB, bsz, seq, LANES)
    kvcmp = _compress(proj4, *_compress_weights(cmp_pos, cmp_w1, cmp_w2))
    o_cmp, sel = _cmp_select(proj4, kvcmp, tabs["cmp"], ovt)
    mix_a = _slcwin(proj4, o_cmp, sel, tabs["slc"], tabs["win"], e_tab, x_tab)
    mix_b = _swa(proj4, sinks, tabs["b"])
    mix_c = _conv(proj4, conv_w)
    o1, lse1 = _dil1(proj4, tabs["d1"])
    mix_d = _dil(proj4, o1, lse1, tabs["d4"], tabs["d16"])
    flat = lambda t: t.reshape(4, bsz * seq, LANES)
    return _outproj(flat(mix_a), flat(mix_b), flat(mix_c), flat(mix_d), w_out.astype(BF16), x2,
                    final_w, final)


def kernel(x, norm_w, w_in, w_out, conv_w, sinks, cmp_pos, cmp_w1, cmp_w2, rel_bias, final_norm_w):
    bsz, seq, _ = x.shape
    depth = norm_w.shape[0]
    tabs = _bias_tables(rel_bias, seq)
    consts = _const_tables(seq)
    x2 = x.reshape(bsz * seq, D_MODEL)
    for layer in range(depth):
        x2 = _layer(x2, bsz, seq, norm_w[layer], w_in[layer], w_out[layer], conv_w[layer], sinks[layer],
                    cmp_pos[layer], cmp_w1[layer], cmp_w2[layer], final_norm_w, layer == depth - 1,
                    tabs, consts)
    return x2.reshape(bsz, seq, D_MODEL)
```

```python
import functools
import math

import numpy as np
import jax
import jax.numpy as jnp
from jax import lax
from jax.experimental import pallas as pl
from jax.experimental.pallas import tpu as pltpu

F32 = jnp.float32
BF16 = jnp.bfloat16

D_MODEL = 2048
HEAD_DIM = 64
N_HEADS = 8
N_KV = 2
CMP_BLK = 32
CMP_STRIDE = 16
CMP_HIDDEN = 128
SLC_BLK = 64
N_SEL = 8
WIN_A = 512
WIN_B = 128
DIL_MAXDIST = 128
NUM_BUCKETS = 32
MAX_DISTANCE = 2048
RMS_EPS = 1e-6
NEG = -1e30
BIG = 1e30
FORCE = 1e4
D_IN = 7192
PHASES = 16

LANES = 128
TILE = 128
VMEM_LIMIT = 56 * 1024 * 1024

SEL_LANE = (HEAD_DIM, 0)
PAD_LANE = (HEAD_DIM + 32, 32)
ONE_LANE = (HEAD_DIM, 0)

CB_AQ = 0
CB_ACMP = 4
CB_AKS, CB_AVS, CB_AKW, CB_AVW = 6, 7, 8, 9
CB_AGATE = 10
CB_BQ, CB_BK, CB_BV, CB_BGATE = 14, 18, 19, 20
CB_CB, CB_CC, CB_CH, CB_CGATE = 24, 28, 32, 36
CB_DQ, CB_DK, CB_DV, CB_DGATE = 40, 44, 48, 52
CB_AGATES = 56
N_CB = 57
CB_PER_STEP = 19

SLC_CHUNK = 4 * TILE
WIN_TILES = 5


def _cparams(*sem):
    return pltpu.CompilerParams(dimension_semantics=sem, vmem_limit_bytes=VMEM_LIMIT)


def _lane():
    return lax.broadcasted_iota(jnp.int32, (1, LANES), 1)


def _lane_lo():
    return _lane() < HEAD_DIM


def _dot_nt(a, b):
    return lax.dot_general(a, b, (((1,), (1,)), ((), ())), preferred_element_type=F32)


def _dot(a, b):
    return jnp.dot(a, b, preferred_element_type=F32)


def _silu(x):
    return x * jax.nn.sigmoid(x)


def _phase_rows(p, n):
    return pl.ds(p, n, stride=PHASES)


def _inproj_kernel(x_ref, nw_ref, w_ref, o_ref):
    x = x_ref[...]
    y = x * lax.rsqrt(jnp.mean(x * x, axis=-1, keepdims=True) + RMS_EPS)
    xn = (y * nw_ref[...]).astype(BF16)
    acc = _dot(xn, w_ref[...])
    for k in range(CB_PER_STEP):
        o_ref[k] = acc[:, k * LANES:(k + 1) * LANES]


def _inproj(x2, norm_w, w_perm, tm=512):
    m = x2.shape[0]
    n_steps = N_CB // CB_PER_STEP
    return pl.pallas_call(
        _inproj_kernel,
        grid=(n_steps, m // tm),
        in_specs=[
            pl.BlockSpec((tm, D_MODEL), lambda j, i: (i, 0)),
            pl.BlockSpec((1, D_MODEL), lambda j, i: (0, 0)),
            pl.BlockSpec((D_MODEL, CB_PER_STEP * LANES), lambda j, i: (0, j)),
        ],
        out_specs=pl.BlockSpec((CB_PER_STEP, tm, LANES), lambda j, i: (j, i, 0)),
        out_shape=jax.ShapeDtypeStruct((N_CB, m, LANES), F32),
        compiler_params=_cparams("arbitrary", "arbitrary"),
        name="inproj",
    )(x2, norm_w.reshape(1, D_MODEL), w_perm)


def _outproj_kernel(a_ref, b_ref, c_ref, d_ref, w_ref, x_ref, fw_ref, o_ref, *, final):
    mix = jnp.concatenate([r[k] for r in (a_ref, b_ref, c_ref, d_ref) for k in range(4)], axis=1)
    y = x_ref[...] + _dot(mix.astype(BF16), w_ref[...])
    if final:
        y = y * lax.rsqrt(jnp.mean(y * y, axis=-1, keepdims=True) + RMS_EPS) * fw_ref[...]
    o_ref[...] = y


def _outproj(mix_a, mix_b, mix_c, mix_d, w_out_bf, x2, final_w, final, tm=512):
    m = x2.shape[0]
    mspec = pl.BlockSpec((4, tm, LANES), lambda i: (0, i, 0))
    return pl.pallas_call(
        functools.partial(_outproj_kernel, final=final),
        grid=(m // tm,),
        in_specs=[mspec, mspec, mspec, mspec,
                  pl.BlockSpec((D_MODEL, D_MODEL), lambda i: (0, 0)),
                  pl.BlockSpec((tm, D_MODEL), lambda i: (i, 0)),
                  pl.BlockSpec((1, D_MODEL), lambda i: (0, 0))],
        out_specs=pl.BlockSpec((tm, D_MODEL), lambda i: (i, 0)),
        out_shape=jax.ShapeDtypeStruct((m, D_MODEL), F32),
        compiler_params=_cparams("arbitrary"),
        name="outproj",
    )(mix_a, mix_b, mix_c, mix_d, w_out_bf, x2, final_w.reshape(1, D_MODEL))


def _pad_feat():
    lane = _lane()
    row = jnp.where((lane == PAD_LANE[0]) | (lane == PAD_LANE[1]), -1.0, 0.0)
    return jnp.broadcast_to(row, (TILE, LANES)).astype(BF16)


def _store_kv_aug(k, v, ka, va, pad_rows, with_blocks):
    lo = _lane_lo()
    lane = _lane()
    seq = k.shape[0]
    blk = lax.broadcasted_iota(jnp.int32, (seq, 1), 0) // SLC_BLK
    for h in range(2):
        own = lo if h == 0 else jnp.logical_not(lo)
        kfeat = jnp.where(lane - SEL_LANE[h] == blk, BIG, 0.0) if with_blocks else 0.0
        vfeat = jnp.where(lane == ONE_LANE[h], 1.0, 0.0)
        ka[h][pad_rows:pad_rows + seq, :] = jnp.where(own, k, kfeat).astype(BF16)
        va[h][pad_rows:pad_rows + seq, :] = jnp.where(own, v, vfeat).astype(BF16)
        flag = jnp.where(lane == PAD_LANE[h], BIG, 0.0)
        ka[h][0:pad_rows, :] = jnp.broadcast_to(flag, (pad_rows, LANES)).astype(BF16)
        va[h][0:pad_rows, :] = jnp.zeros((pad_rows, LANES), BF16)


def _group_both_halves(g, x):
    own = jnp.where(_lane_lo() == (g == 0), x, 0.0)
    return own + pltpu.roll(own, HEAD_DIM, axis=1)


def _attend(q_list, qfeat, i, n_chunks, kc, pad_rows, ka, va, bias_fn):
    lo = _lane_lo()
    ncb = len(q_list)
    nt = kc // TILE
    qa = [jnp.concatenate([jnp.where(lo, q, qfeat) for q in q_list], axis=0),
          jnp.concatenate([jnp.where(lo, qfeat, q) for q in q_list], axis=0)]

    def step(k, carry):
        ms, accs = carry
        start = pl.multiple_of(pad_rows + (i + 1) * TILE - (k + 1) * kc, TILE)
        new_ms = [None] * (2 * ncb)
        new_accs = []
        for h in range(2):
            s = _dot_nt(qa[h], ka[h][pl.ds(start, kc), :])
            ps, alphas = [], []
            for cb in range(ncb):
                blks = [s[cb * TILE:(cb + 1) * TILE, jt * TILE:(jt + 1) * TILE]
                        + bias_fn(2 * cb + h, k * nt + (nt - 1 - jt)) for jt in range(nt)]
                mx = blks[0]
                for b in blks[1:]:
                    mx = jnp.maximum(mx, b)
                m_old = ms[2 * cb + h]
                m_new = jnp.maximum(m_old, jnp.max(mx, axis=-1, keepdims=True))
                alphas.append(jnp.broadcast_to(jnp.exp(m_old - m_new), (TILE, LANES)))
                ps.append(jnp.concatenate([jnp.exp(b - m_new).astype(BF16) for b in blks], axis=1))
                new_ms[2 * cb + h] = m_new
            p = jnp.concatenate(ps, axis=0)
            alpha = jnp.concatenate(alphas, axis=0)
            new_accs.append(accs[h] * alpha + _dot(p, va[h][pl.ds(start, kc), :]))
        return tuple(new_ms), tuple(new_accs)

    init = (tuple(jnp.full((TILE, 1), NEG, F32) for _ in range(2 * ncb)),
            tuple(jnp.zeros((ncb * TILE, LANES), F32) for _ in range(2)))
    if n_chunks is None:
        return step(0, init)
    return lax.fori_loop(0, n_chunks, step, init)


def _split_acc(accs, cb):
    a0 = accs[0][cb * TILE:(cb + 1) * TILE, :]
    a1 = accs[1][cb * TILE:(cb + 1) * TILE, :]
    return a0, a0[:, ONE_LANE[0]:ONE_LANE[0] + 1], a1, a1[:, ONE_LANE[1]:ONE_LANE[1] + 1]


def _swa_kernel(sink_ref, q_ref, k_ref, v_ref, gate_ref, bias_ref, o_ref, ka0, ka1, va0, va1, *, seq):
    g = pl.program_id(1)
    lo = _lane_lo()
    ka, va = (ka0, ka1), (va0, va1)
    _store_kv_aug(_group_both_halves(g, k_ref[...]), _group_both_halves(g, v_ref[...]), ka, va, TILE, False)
    qfeat = _pad_feat()

    def qtile(i, _):
        rows = pl.ds(pl.multiple_of(i * TILE, TILE), TILE)
        q_list = [(q_ref[cb, rows, :] * (HEAD_DIM ** -0.5)).astype(BF16) for cb in range(2)]
        ms, accs = _attend(q_list, qfeat, i, None, 2 * TILE, TILE, ka, va,
                           lambda h, d: bias_ref[h, d])
        for cb in range(2):
            a0, l0, a1, l1 = _split_acc(accs, cb)
            outs = []
            for h, (a, l) in enumerate(((a0, l0), (a1, l1))):
                m = ms[2 * cb + h]
                sink = sink_ref[g * 4 + 2 * cb + h]
                mt = jnp.maximum(m, sink)
                scale = jnp.exp(m - mt)
                outs.append(a * (scale / (l * scale + jnp.exp(sink - mt))))
            o = jnp.where(lo, outs[0], outs[1])
            o_ref[cb, rows, :] = o * _silu(gate_ref[cb, rows, :])
        return 0

    lax.fori_loop(0, seq // TILE, qtile, 0)


def _swa(proj4, sinks, t_b):
    _, bsz, seq, _ = proj4.shape
    return pl.pallas_call(
        functools.partial(_swa_kernel, seq=seq),
        grid=(bsz, N_KV),
        in_specs=[
            pl.BlockSpec(memory_space=pltpu.SMEM),
            pl.BlockSpec((2, None, seq, LANES), lambda b, g: (CB_BQ // 2 + g, b, 0, 0)),
            pl.BlockSpec((None, None, seq, LANES), lambda b, g: (CB_BK, b, 0, 0)),
            pl.BlockSpec((None, None, seq, LANES), lambda b, g: (CB_BV, b, 0, 0)),
            pl.BlockSpec((2, None, seq, LANES), lambda b, g: (CB_BGATE // 2 + g, b, 0, 0)),
            pl.BlockSpec((4, 2, TILE, TILE), lambda b, g: (g, 0, 0, 0)),
        ],
        out_specs=pl.BlockSpec((2, None, seq, LANES), lambda b, g: (g, b, 0, 0)),
        out_shape=jax.ShapeDtypeStruct((4, bsz, seq, LANES), F32),
        scratch_shapes=[pltpu.VMEM((seq + TILE, LANES), BF16)] * 4,
        compiler_params=_cparams("arbitrary", "arbitrary"),
        name="swa",
    )(sinks, proj4, proj4, proj4, proj4, t_b)


def _conv_kernel(cw_ref, b_ref, c_ref, h_ref, gate_ref, o_ref, pad_ref, *, seq):
    pad_ref[0:8, :] = jnp.zeros((8, LANES), F32)
    pad_ref[8:8 + seq, :] = c_ref[...] * h_ref[...]
    w = cw_ref[...]
    y = (w[0:1, :] * pad_ref[6:6 + seq, :] + w[1:2, :] * pad_ref[7:7 + seq, :]
         + w[2:3, :] * pad_ref[8:8 + seq, :])
    o_ref[...] = b_ref[...] * y * _silu(gate_ref[...])


def _conv(proj4, conv_w):
    _, bsz, seq, _ = proj4.shape
    cw = jnp.pad(conv_w, ((0, 5), (0, 0)))

    def spec(base):
        return pl.BlockSpec((None, None, seq, LANES), lambda b, c: (base + c, b, 0, 0))

    return pl.pallas_call(
        functools.partial(_conv_kernel, seq=seq),
        grid=(bsz, 4),
        in_specs=[pl.BlockSpec((8, LANES), lambda b, c: (0, c)),
                  spec(CB_CB), spec(CB_CC), spec(CB_CH), spec(CB_CGATE)],
        out_specs=pl.BlockSpec((None, None, seq, LANES), lambda b, c: (c, b, 0, 0)),
        out_shape=jax.ShapeDtypeStruct((4, bsz, seq, LANES), F32),
        scratch_shapes=[pltpu.VMEM((seq + 8, LANES), F32)],
        compiler_params=_cparams("arbitrary", "arbitrary"),
        name="conv",
    )(cw, proj4, proj4, proj4, proj4)


def _dil1_kernel(q_ref, k_ref, v_ref, bias_ref, o_ref, lse_ref, ka0, ka1, va0, va1, *, seq):
    lo = _lane_lo()
    ka, va = (ka0, ka1), (va0, va1)
    _store_kv_aug(k_ref[...], v_ref[...], ka, va, TILE, False)
    qfeat = _pad_feat()

    def qtile(i, _):
        rows = pl.ds(pl.multiple_of(i * TILE, TILE), TILE)
        q2 = (q_ref[rows, :] * (HEAD_DIM ** -0.5)).astype(BF16)
        ms, accs = _attend([q2], qfeat, i, None, 2 * TILE, TILE, ka, va, lambda h, d: bias_ref[h, d])
        a0, l0, a1, l1 = _split_acc(accs, 0)
        o_ref[rows, :] = jnp.where(lo, a0 / l0, a1 / l1)
        lse_ref[rows, :] = jnp.where(lo, ms[0] + jnp.log(l0), ms[1] + jnp.log(l1))
        return 0

    lax.fori_loop(0, seq // TILE, qtile, 0)


def _dil1(proj4, t_d1):
    _, bsz, seq, _ = proj4.shape

    def spec(base):
        return pl.BlockSpec((None, None, seq, LANES), lambda b, h: (base + h, b, 0, 0))

    ospec = pl.BlockSpec((None, None, seq, LANES), lambda b, h: (h, b, 0, 0))
    oshape = jax.ShapeDtypeStruct((4, bsz, seq, LANES), F32)
    return pl.pallas_call(
        functools.partial(_dil1_kernel, seq=seq),
        grid=(bsz, 4),
        in_specs=[spec(CB_DQ), spec(CB_DK), spec(CB_DV),
                  pl.BlockSpec((None, 2, 2, TILE, TILE), lambda b, h: (h, 0, 0, 0, 0))],
        out_specs=(ospec, ospec),
        out_shape=(oshape, oshape),
        scratch_shapes=[pltpu.VMEM((seq + TILE, LANES), BF16)] * 4,
        compiler_params=_cparams("arbitrary", "arbitrary"),
        name="dil1",
    )(proj4, proj4, proj4, t_d1)


def _dil_kernel(q_ref, k_ref, v_ref, o1_ref, lse1_ref, gate_ref, t4_ref, t16_ref, o_ref,
                qs, kp0, kp1, vp0, vp1, o4s, l4s, o16s, l16s, *, seq):
    lo = _lane_lo()
    na = seq // PHASES
    kp, vp = (kp0, kp1), (vp0, vp1)
    for p in range(PHASES):
        rows = _phase_rows(p, na)
        qs[p] = (q_ref[rows, :] * (HEAD_DIM ** -0.5)).astype(BF16)
        k = k_ref[rows, :]
        v = v_ref[rows, :]
        kp0[p] = jnp.where(lo, k, 0.0).astype(BF16)
        kp1[p] = jnp.where(lo, 0.0, k).astype(BF16)
        vp0[p] = jnp.where(lo, v, 0.0).astype(BF16)
        vp1[p] = jnp.where(lo, 0.0, v).astype(BF16)

    def residue(r, _):
        phases = [4 * c + r for c in range(4)]
        for p in phases:
            outs, lses = [], []
            for h in range(2):
                s = _dot_nt(qs[p], kp[h][p]) + t16_ref[h]
                m = jnp.max(s, axis=-1, keepdims=True)
                e = jnp.exp(s - m)
                l = jnp.sum(e, axis=-1, keepdims=True)
                outs.append(_dot(e.astype(BF16), vp[h][p]) / l)
                lses.append(m + jnp.log(l))
            o16s[p] = outs[0] + outs[1]
            l16s[p] = jnp.where(lo, lses[0], lses[1])
        q = jnp.concatenate([qs[p] for p in phases], axis=0)
        outs, lses = [], []
        for h in range(2):
            kk = jnp.concatenate([kp[h][p] for p in phases], axis=0)
            vv = jnp.concatenate([vp[h][p] for p in phases], axis=0)
            s = _dot_nt(q, kk)
            es, ls, ms_ = [], [], []
            for c in range(4):
                blks = [s[c * na:(c + 1) * na, c2 * na:(c2 + 1) * na] + t4_ref[h, c - c2 + 3]
                        for c2 in range(4)]
                mx = jnp.maximum(jnp.maximum(blks[0], blks[1]), jnp.maximum(blks[2], blks[3]))
                m = jnp.max(mx, axis=-1, keepdims=True)
                eb = [jnp.exp(b - m) for b in blks]
                l = jnp.sum(eb[0] + eb[1] + eb[2] + eb[3], axis=-1, keepdims=True)
                es.append(jnp.concatenate([e.astype(BF16) for e in eb], axis=1))
                ls.append(l)
                ms_.append(m)
            pv = _dot(jnp.concatenate(es, axis=0), vv)
            outs.append([pv[c * na:(c + 1) * na, :] / ls[c] for c in range(4)])
            lses.append([ms_[c] + jnp.log(ls[c]) for c in range(4)])
        for c, p in enumerate(phases):
            o4s[p] = outs[0][c] + outs[1][c]
            l4s[p] = jnp.where(lo, lses[0][c], lses[1][c])
        return 0

    lax.fori_loop(0, 4, residue, 0)

    for p in range(PHASES):
        rows = _phase_rows(p, na)
        la, lb, lc = lse1_ref[rows, :], l4s[p], l16s[p]
        mx = jnp.maximum(jnp.maximum(la, lb), lc)
        ea, eb, ec = jnp.exp(la - mx), jnp.exp(lb - mx), jnp.exp(lc - mx)
        o = (ea * o1_ref[rows, :] + eb * o4s[p] + ec * o16s[p]) / (ea + eb + ec)
        o_ref[rows, :] = o * _silu(gate_ref[rows, :])


def _dil(proj4, o1, lse1, t_d4, t_d16):
    _, bsz, seq, _ = proj4.shape
    na = seq // PHASES

    def spec(base):
        return pl.BlockSpec((None, None, seq, LANES), lambda b, h: (base + h, b, 0, 0))

    return pl.pallas_call(
        functools.partial(_dil_kernel, seq=seq),
        grid=(bsz, 4),
        in_specs=[spec(CB_DQ), spec(CB_DK), spec(CB_DV), spec(0), spec(0), spec(CB_DGATE),
                  pl.BlockSpec((None, 2, 7, TILE, TILE), lambda b, h: (h, 0, 0, 0, 0)),
                  pl.BlockSpec((None, 2, TILE, TILE), lambda b, h: (h, 0, 0, 0))],
        out_specs=spec(0),
        out_shape=jax.ShapeDtypeStruct((4, bsz, seq, LANES), F32),
        scratch_shapes=[pltpu.VMEM((PHASES, na, LANES), BF16)] * 5 + [pltpu.VMEM((PHASES, na, LANES), F32)] * 4,
        compiler_params=_cparams("arbitrary", "arbitrary"),
        name="dil",
    )(proj4, proj4, proj4, o1, lse1, proj4, t_d4, t_d16)


def _compress_kernel(kv_ref, pos_ref, wlo_ref, whi_ref, w2_ref, o_ref, *, seq):
    na = seq // PHASES
    r = jnp.concatenate([kv_ref[_phase_rows(p, na), :] for p in range(PHASES)], axis=1)
    h_lo = _dot((r + pos_ref[0:1, :]).astype(BF16), wlo_ref[...])
    h_hi = _dot((r + pos_ref[1:2, :]).astype(BF16), whi_ref[...])
    h = h_lo + pltpu.roll(h_hi, na - 1, axis=0)
    o_ref[...] = _dot(_silu(h).astype(BF16), w2_ref[...])


def _compress(proj4, pos2, w_lo, w_hi, w2):
    _, bsz, seq, _ = proj4.shape
    na = seq // PHASES
    wide = PHASES * LANES
    full = lambda shape: pl.BlockSpec(shape, lambda b, g: (0,) * len(shape))
    return pl.pallas_call(
        functools.partial(_compress_kernel, seq=seq),
        grid=(bsz, N_KV),
        in_specs=[pl.BlockSpec((None, None, seq, LANES), lambda b, g: (CB_ACMP + g, b, 0, 0)),
                  full((2, wide)), full((wide, 2 * CMP_HIDDEN)), full((wide, 2 * CMP_HIDDEN)),
                  full((2 * CMP_HIDDEN, LANES))],
        out_specs=pl.BlockSpec((None, None, na, LANES), lambda b, g: (b, g, 0, 0)),
        out_shape=jax.ShapeDtypeStruct((bsz, N_KV, na, LANES), F32),
        compiler_params=_cparams("arbitrary", "arbitrary"),
        name="compress",
    )(proj4, pos2, w_lo, w_hi, w2)


def _cmp_kernel(q_ref, kv_ref, bias_ref, ovt_ref, o_ref, feat_ref, *, seq):
    lo = _lane_lo()
    kv = kv_ref[...]
    k_lo = jnp.where(lo, kv, 0.0)
    v_hi = jnp.where(lo, 0.0, kv)
    kp = (k_lo.astype(BF16), pltpu.roll(k_lo, HEAD_DIM, axis=1).astype(BF16))
    vp = (pltpu.roll(v_hi, HEAD_DIM, axis=1).astype(BF16), v_hi.astype(BF16))
    ovt = ovt_ref[...]
    n_slc = ovt.shape[0]
    blk = lax.broadcasted_iota(jnp.int32, (n_slc, TILE), 0)
    pad_rows = jnp.where(blk == 0, -1.0, 0.0).astype(BF16)
    eye = (lax.broadcasted_iota(jnp.int32, (TILE, TILE), 0)
           == lax.broadcasted_iota(jnp.int32, (TILE, TILE), 1)).astype(BF16)

    def qtile(i, _):
        rows = pl.ds(pl.multiple_of(i * TILE, TILE), TILE)
        psum = jnp.zeros((TILE, LANES), F32)
        for cb in range(2):
            q2 = (q_ref[cb, rows, :] * (HEAD_DIM ** -0.5)).astype(BF16)
            o2 = jnp.zeros((TILE, LANES), F32)
            for hh in range(2):
                s = _dot_nt(q2, kp[hh]) + bias_ref[2 * cb + hh, rows, :]
                m = jnp.maximum(jnp.max(s, axis=-1, keepdims=True), -1e20)
                e = jnp.exp(s - m)
                p = e / jnp.maximum(jnp.sum(e, axis=-1, keepdims=True), 1e-30)
                psum = psum + p
                o2 = o2 + _dot(p.astype(BF16), vp[hh])
            o_ref[cb, rows, :] = o2
        p1 = psum.astype(BF16)
        rem = psum - p1.astype(F32)
        p2 = rem.astype(BF16)
        p3 = (rem - p2.astype(F32)).astype(BF16)
        imp = _dot_nt(ovt, p1) + _dot_nt(ovt, p2) + _dot_nt(ovt, p3)
        t = i * TILE + lax.broadcasted_iota(jnp.int32, (n_slc, TILE), 1)
        cur = t // SLC_BLK
        forced = (blk == 0) | (blk == cur) | (blk == cur - 1)
        imp = jnp.where(blk > cur, NEG, jnp.where(forced, FORCE, imp))
        rank = jnp.zeros((n_slc, TILE), F32)
        for mth in range(n_slc):
            row = imp[mth:mth + 1, :]
            tie = (blk > mth).astype(F32)
            rank = rank + jnp.where(row > imp, 1.0, jnp.where(row == imp, tie, 0.0))
        unsel = jnp.where(rank < N_SEL, 0.0, -1.0).astype(BF16)
        half = jnp.concatenate([unsel, pad_rows], axis=0)
        feat_t = jnp.concatenate([half, half], axis=0)
        feat_ref[rows, :] = _dot_nt(eye, feat_t).astype(BF16)
        return 0

    lax.fori_loop(0, seq // TILE, qtile, 0)


def _cmp_select(proj4, kvcmp, t_cmp, ovt):
    _, bsz, seq, _ = proj4.shape
    return pl.pallas_call(
        functools.partial(_cmp_kernel, seq=seq),
        grid=(bsz, N_KV),
        in_specs=[pl.BlockSpec((2, None, seq, LANES), lambda b, g: (CB_AQ // 2 + g, b, 0, 0)),
                  pl.BlockSpec((None, None, TILE, LANES), lambda b, g: (b, g, 0, 0)),
                  pl.BlockSpec((4, seq, LANES), lambda b, g: (g, 0, 0)),
                  pl.BlockSpec(ovt.shape, lambda b, g: (0, 0))],
        out_specs=(pl.BlockSpec((2, None, seq, LANES), lambda b, g: (g, b, 0, 0)),
                   pl.BlockSpec((None, None, seq, LANES), lambda b, g: (b, g, 0, 0))),
        out_shape=(jax.ShapeDtypeStruct((4, bsz, seq, LANES), F32),
                   jax.ShapeDtypeStruct((bsz, N_KV, seq, LANES), BF16)),
        compiler_params=_cparams("arbitrary", "arbitrary"),
        name="cmp_select",
    )(proj4, kvcmp, t_cmp, ovt)


def _slcwin_kernel(q_ref, ks_ref, vs_ref, kw_ref, vw_ref, ocmp_ref, feat_ref, gates_ref, agate_ref,
                   tslc_ref, twin_ref, x_ref, o_ref,
                   ksa0, ksa1, vsa0, vsa1, kwa0, kwa1, vwa0, vwa1, *, seq):
    g = pl.program_id(1)
    lo = _lane_lo()
    ksa, vsa, kwa, vwa = (ksa0, ksa1), (vsa0, vsa1), (kwa0, kwa1), (vwa0, vwa1)
    slc_pad = SLC_CHUNK - TILE
    win_pad = (WIN_TILES - 1) * TILE
    _store_kv_aug(_group_both_halves(g, ks_ref[...]), _group_both_halves(g, vs_ref[...]),
                  ksa, vsa, slc_pad, True)
    _store_kv_aug(_group_both_halves(g, kw_ref[...]), _group_both_halves(g, vw_ref[...]),
                  kwa, vwa, win_pad, False)
    slc_tiles = SLC_CHUNK // TILE

    def qtile(i, _):
        rows = pl.ds(pl.multiple_of(i * TILE, TILE), TILE)
        qfeat = feat_ref[rows, :]
        sg = jax.nn.sigmoid(gates_ref[rows, :])
        sg1 = sg.astype(BF16)
        sg2 = (sg - sg1.astype(F32)).astype(BF16)
        q_list = [(q_ref[cb, rows, :] * (HEAD_DIM ** -0.5)).astype(BF16) for cb in range(2)]
        _, acc_s = _attend(q_list, qfeat, i, (i + slc_tiles) // slc_tiles, SLC_CHUNK, slc_pad, ksa, vsa,
                           lambda h, d: tslc_ref[h, d])
        _, acc_w = _attend(q_list, qfeat, i, None, WIN_TILES * TILE, win_pad, kwa, vwa,
                           lambda h, d: twin_ref[h, d])
        for cb in range(2):
            a0, l0, a1, l1 = _split_acc(acc_s, cb)
            o_slc = jnp.where(lo, a0 / l0, a1 / l1)
            a0, l0, a1, l1 = _split_acc(acc_w, cb)
            o_win = jnp.where(lo, a0 / l0, a1 / l1)
            gts = [_dot(sg1, x_ref[br, cb]) + _dot(sg2, x_ref[br, cb]) for br in range(3)]
            o = gts[0] * ocmp_ref[cb, rows, :] + gts[1] * o_slc + gts[2] * o_win
            o_ref[cb, rows, :] = o * _silu(agate_ref[cb, rows, :])
        return 0

    lax.fori_loop(0, seq // TILE, qtile, 0)


def _slcwin(proj4, o_cmp, feat, t_slc, t_win, x_tab):
    _, bsz, seq, _ = proj4.shape
    nt = seq // TILE

    def one(cb):
        return pl.BlockSpec((None, None, seq, LANES), lambda b, g: (cb, b, 0, 0))

    def two(base):
        return pl.BlockSpec((2, None, seq, LANES), lambda b, g: (base // 2 + g, b, 0, 0))

    slc_rows = seq + SLC_CHUNK - TILE
    win_rows = seq + (WIN_TILES - 1) * TILE
    return pl.pallas_call(
        functools.partial(_slcwin_kernel, seq=seq),
        grid=(bsz, N_KV),
        in_specs=[two(CB_AQ), one(CB_AKS), one(CB_AVS), one(CB_AKW), one(CB_AVW),
                  pl.BlockSpec((2, None, seq, LANES), lambda b, g: (g, b, 0, 0)),
                  pl.BlockSpec((None, None, seq, LANES), lambda b, g: (b, g, 0, 0)),
                  one(CB_AGATES), two(CB_AGATE),
                  pl.BlockSpec((4, nt, TILE, TILE), lambda b, g: (g, 0, 0, 0)),
                  pl.BlockSpec((4, WIN_TILES, TILE, TILE), lambda b, g: (g, 0, 0, 0)),
                  pl.BlockSpec((None, 3, 2, TILE, TILE), lambda b, g: (g, 0, 0, 0, 0))],
        out_specs=pl.BlockSpec((2, None, seq, LANES), lambda b, g: (g, b, 0, 0)),
        out_shape=jax.ShapeDtypeStruct((4, bsz, seq, LANES), F32),
        scratch_shapes=[pltpu.VMEM((slc_rows, LANES), BF16)] * 4 + [pltpu.VMEM((win_rows, LANES), BF16)] * 4,
        compiler_params=_cparams("arbitrary", "arbitrary"),
        name="slcwin",
    )(proj4, proj4, proj4, proj4, proj4, o_cmp, feat, proj4, proj4, t_slc, t_win, x_tab)


def _toeplitz_kernel(par_ref, v_ref, o_ref, *, n_tiles, interleave):
    a = lax.broadcasted_iota(jnp.int32, (TILE, TILE), 0)
    b = lax.broadcasted_iota(jnp.int32, (TILE, TILE), 1)
    for t in range(n_tiles):
        x = jnp.broadcast_to(v_ref[t:t + 1, :], (TILE, 2 * TILE))
        y = pltpu.roll(x, 0, 1, stride=1, stride_axis=0)[:, :TILE]
        dist = par_ref[t, 0] + par_ref[t, 1] * (a - b)
        ok = (dist >= par_ref[t, 2]) & (dist <= par_ref[t, 3]) & (b < par_ref[t, 4])
        tile = jnp.where(ok, y, NEG)
        if interleave:
            o_ref[pl.ds(t, TILE, stride=n_tiles), :] = tile
        else:
            o_ref[t] = tile


def _toeplitz(v, params, interleave=False):
    heads, n_tiles, _ = v.shape
    if interleave:
        oshape, ospec = (heads, n_tiles * TILE, TILE), pl.BlockSpec((None, n_tiles * TILE, TILE), lambda h: (h, 0, 0))
    else:
        oshape, ospec = (heads, n_tiles, TILE, TILE), pl.BlockSpec((None, n_tiles, TILE, TILE), lambda h: (h, 0, 0, 0))
    return pl.pallas_call(
        functools.partial(_toeplitz_kernel, n_tiles=n_tiles, interleave=interleave),
        grid=(heads,),
        in_specs=[pl.BlockSpec(memory_space=pltpu.SMEM),
                  pl.BlockSpec((None, n_tiles, 2 * TILE), lambda h: (h, 0, 0))],
        out_specs=ospec,
        out_shape=jax.ShapeDtypeStruct(oshape, F32),
        compiler_params=_cparams("arbitrary"),
        name="toeplitz",
    )(jnp.asarray(params, jnp.int32), v)


def _t5_bucket(d):
    exact = NUM_BUCKETS // 2
    large = exact + (jnp.log(jnp.maximum(d, exact).astype(F32) / exact)
                     / math.log(MAX_DISTANCE / exact) * (NUM_BUCKETS - exact)).astype(jnp.int32)
    return jnp.where(d < exact, d, jnp.minimum(large, NUM_BUCKETS - 1))


def _generator_rows(ext, off, base, mul):
    lo = off + base - 127 * mul
    first = ext[:, lo:lo + 127 * mul + 1:mul][:, ::-1]
    second = ext[:, off + base + mul:off + base + 128 * mul + 1:mul][:, ::-1]
    return jnp.concatenate([first, second], axis=1)


def _bias_tables(rel_bias, seq):
    big = 1 << 30
    onehot = jax.nn.one_hot(_t5_bucket(jnp.arange(seq, dtype=jnp.int32)), NUM_BUCKETS, dtype=F32)
    bias_d = jnp.dot(onehot, rel_bias, precision=lax.Precision.HIGHEST).T
    off = 128 * PHASES + 64
    ext = jnp.pad(bias_d, ((0, 0), (off, off)))
    ba, bb, bd = ext[:8], ext[8:16], ext[16:24]
    nt = seq // TILE

    def plain(e, n, hi):
        v = jnp.stack([_generator_rows(e, off, TILE * d, 1) for d in range(n)], axis=1)
        return _toeplitz(v, [(TILE * d, 1, 0, hi, TILE) for d in range(n)])

    t_slc = plain(ba, nt, big)
    t_win = plain(ba, WIN_TILES, WIN_A - 1)
    t_b = plain(bb, 2, WIN_B - 1)
    t_d1 = plain(bd, 2, DIL_MAXDIST)
    v4 = jnp.stack([_generator_rows(bd, off, 4 * dl, 16) for dl in range(-3, 4)], axis=1)
    t_d4 = _toeplitz(v4, [(dl, 4, 0, DIL_MAXDIST, TILE) for dl in range(-3, 4)])
    v16 = jnp.stack([_generator_rows(bd, off, 0, 16)], axis=1)
    t_d16 = _toeplitz(v16, [(0, 1, 0, big, TILE)])
    n_cmp = (seq - CMP_BLK) // CMP_STRIDE + 1
    vc = jnp.stack([_generator_rows(ba, off, p - (CMP_BLK - 1), CMP_STRIDE) for p in range(PHASES)], axis=1)
    t_cmp = _toeplitz(vc, [(p - (CMP_BLK - 1), CMP_STRIDE, 0, big, n_cmp) for p in range(PHASES)],
                      interleave=True)
    return dict(slc=t_slc, win=t_win, b=t_b, d1=t_d1.reshape(4, 2, 2, TILE, TILE),
                d4=t_d4.reshape(4, 2, 7, TILE, TILE), d16=t_d16.reshape(4, 2, TILE, TILE), cmp=t_cmp)


def _const_tables(seq):
    n_cmp = (seq - CMP_BLK) // CMP_STRIDE + 1
    n_slc = seq // SLC_BLK
    c0 = np.arange(TILE)[None, :] * CMP_STRIDE
    s0 = np.arange(n_slc)[:, None] * SLC_BLK
    ovt = ((c0 < s0 + SLC_BLK) & (c0 + CMP_BLK > s0) & (np.arange(TILE)[None, :] < n_cmp))
    row = np.arange(TILE)[None, None, None, :, None]
    lane = np.arange(TILE)[None, None, None, None, :]
    g = np.arange(N_KV)[:, None, None, None, None]
    br = np.arange(3)[None, :, None, None, None]
    cb = np.arange(2)[None, None, :, None, None]
    x_tab = (row == br * N_HEADS + 4 * g + 2 * cb + lane // HEAD_DIM)
    as_bf = lambda m: jnp.asarray(m.astype(np.float32), BF16)
    return as_bf(ovt), as_bf(x_tab)


def _permute_w_in(w):
    pad = jnp.zeros((w.shape[0], N_CB * LANES - D_IN), w.dtype)
    return jnp.concatenate([
        w[:, 0:512],
        w[:, 512:576], w[:, 640:704], w[:, 576:640], w[:, 704:768],
        w[:, 768:1280],
        w[:, 1304:D_IN],
        w[:, 1280:1304], pad], axis=1).astype(BF16)


def _compress_weights(cmp_pos, cmp_w1, cmp_w2):
    half = CMP_BLK // 2
    w1 = cmp_w1.reshape(2, CMP_BLK, HEAD_DIM, CMP_HIDDEN)
    z = jnp.zeros((half, HEAD_DIM, CMP_HIDDEN), F32)

    def stack(part):
        top = jnp.concatenate([w1[0, part], z], axis=-1)
        bot = jnp.concatenate([z, w1[1, part]], axis=-1)
        return jnp.concatenate([top, bot], axis=1).reshape(half * LANES, 2 * CMP_HIDDEN).astype(BF16)

    w_lo, w_hi = stack(slice(0, half)), stack(slice(half, CMP_BLK))
    z2 = jnp.zeros((CMP_HIDDEN, HEAD_DIM), F32)
    w2 = jnp.concatenate([jnp.concatenate([cmp_w2[0], z2], axis=1),
                          jnp.concatenate([z2, cmp_w2[1]], axis=1)], axis=0).astype(BF16)
    pos = jnp.concatenate([cmp_pos[0], cmp_pos[1]], axis=-1)
    pos2 = pos.reshape(2, half * LANES)
    return pos2, w_lo, w_hi, w2


def _layer(x2, bsz, seq, norm_w, w_in, w_out, conv_w, sinks, cmp_pos, cmp_w1, cmp_w2, final_w, final,
           tabs, consts):
    ovt, x_tab = consts
    proj = _inproj(x2, norm_w, _permute_w_in(w_in))
    proj4 = proj.reshape(N_CB, bsz, seq, LANES)
    kvcmp = _compress(proj4, *_compress_weights(cmp_pos, cmp_w1, cmp_w2))
    o_cmp, feat = _cmp_select(proj4, kvcmp, tabs["cmp"], ovt)
    mix_a = _slcwin(proj4, o_cmp, feat, tabs["slc"], tabs["win"], x_tab)
    mix_b = _swa(proj4, sinks, tabs["b"])
    mix_c = _conv(proj4, conv_w)
    o1, lse1 = _dil1(proj4, tabs["d1"])
    mix_d = _dil(proj4, o1, lse1, tabs["d4"], tabs["d16"])
    flat = lambda t: t.reshape(4, bsz * seq, LANES)
    return _outproj(flat(mix_a), flat(mix_b), flat(mix_c), flat(mix_d), w_out.astype(BF16), x2,
                    final_w, final)


def kernel(x, norm_w, w_in, w_out, conv_w, sinks, cmp_pos, cmp_w1, cmp_w2, rel_bias, final_norm_w):
    bsz, seq, _ = x.shape
    depth = norm_w.shape[0]
    tabs = _bias_tables(rel_bias, seq)
    consts = _const_tables(seq)
    x2 = x.reshape(bsz * seq, D_MODEL)
    for layer in range(depth):
        x2 = _layer(x2, bsz, seq, norm_w[layer], w_in[layer], w_out[layer], conv_w[layer], sinks[layer],
                    cmp_pos[layer], cmp_w1[layer], cmp_w2[layer], final_norm_w, layer == depth - 1,
                    tabs, consts)
    return x2.reshape(bsz, seq, D_MODEL)
```

```python
import functools
import math

import numpy as np
import jax
import jax.numpy as jnp
from jax import lax
from jax.experimental import pallas as pl
from jax.experimental.pallas import tpu as pltpu

F32 = jnp.float32
BF16 = jnp.bfloat16

D_MODEL = 2048
HEAD_DIM = 64
N_HEADS = 8
N_KV = 2
CMP_BLK = 32
CMP_STRIDE = 16
CMP_HIDDEN = 128
SLC_BLK = 64
N_SEL = 8
WIN_A = 512
WIN_B = 128
DIL_MAXDIST = 128
NUM_BUCKETS = 32
MAX_DISTANCE = 2048
RMS_EPS = 1e-6
NEG = -1e30
BIG = 1e30
FORCE = 1e4
D_IN = 7192
PHASES = 16

LANES = 128
TILE = 128
VMEM_LIMIT = 56 * 1024 * 1024

SEL_LANE = (HEAD_DIM, 0)
PAD_LANE = (HEAD_DIM + 32, 32)
ONE_LANE = (HEAD_DIM, 0)

CB_AQ = 0
CB_ACMP = 4
CB_AKS, CB_AVS, CB_AKW, CB_AVW = 6, 7, 8, 9
CB_AGATE = 10
CB_BQ, CB_BK, CB_BV, CB_BGATE = 14, 18, 19, 20
CB_CB, CB_CC, CB_CH, CB_CGATE = 24, 28, 32, 36
CB_DQ, CB_DK, CB_DV, CB_DGATE = 40, 44, 48, 52
CB_AGATES = 56
N_CB = 57
CB_PER_STEP = 19

SLC_CHUNK = 4 * TILE
WIN_TILES = 5
BAND_STREAMS = 4
SLC_STREAMS = 2
CMP_ROWS = 4 * TILE


def _cparams(*sem):
    return pltpu.CompilerParams(dimension_semantics=sem, vmem_limit_bytes=VMEM_LIMIT)


def _lane():
    return lax.broadcasted_iota(jnp.int32, (1, LANES), 1)


def _lane_lo():
    return _lane() < HEAD_DIM


def _dot_nt(a, b):
    return lax.dot_general(a, b, (((1,), (1,)), ((), ())), preferred_element_type=F32)


def _dot(a, b):
    return jnp.dot(a, b, preferred_element_type=F32)


def _silu(x):
    return x * jax.nn.sigmoid(x)


def _inproj_kernel(x_ref, nw_ref, w_ref, o_ref):
    x = x_ref[...]
    y = x * lax.rsqrt(jnp.mean(x * x, axis=-1, keepdims=True) + RMS_EPS)
    xn = (y * nw_ref[...]).astype(BF16)
    acc = _dot(xn, w_ref[...])
    for k in range(CB_PER_STEP):
        o_ref[k] = acc[:, k * LANES:(k + 1) * LANES]


def _inproj(x2, norm_w, w_perm, tm=512):
    m = x2.shape[0]
    n_steps = N_CB // CB_PER_STEP
    return pl.pallas_call(
        _inproj_kernel,
        grid=(n_steps, m // tm),
        in_specs=[
            pl.BlockSpec((tm, D_MODEL), lambda j, i: (i, 0)),
            pl.BlockSpec((1, D_MODEL), lambda j, i: (0, 0)),
            pl.BlockSpec((D_MODEL, CB_PER_STEP * LANES), lambda j, i: (0, j)),
        ],
        out_specs=pl.BlockSpec((CB_PER_STEP, tm, LANES), lambda j, i: (j, i, 0)),
        out_shape=jax.ShapeDtypeStruct((N_CB, m, LANES), F32),
        compiler_params=_cparams("arbitrary", "arbitrary"),
        name="inproj",
    )(x2, norm_w.reshape(1, D_MODEL), w_perm)


def _outproj_kernel(a_ref, b_ref, c_ref, d_ref, w_ref, x_ref, fw_ref, o_ref, *, final):
    mix = jnp.concatenate([r[k] for r in (a_ref, b_ref, c_ref, d_ref) for k in range(4)], axis=1)
    y = x_ref[...] + _dot(mix.astype(BF16), w_ref[...])
    if final:
        y = y * lax.rsqrt(jnp.mean(y * y, axis=-1, keepdims=True) + RMS_EPS) * fw_ref[...]
    o_ref[...] = y


def _outproj(mix_a, mix_b, mix_c, mix_d, w_out_bf, x2, final_w, final, tm=512):
    m = x2.shape[0]
    mspec = pl.BlockSpec((4, tm, LANES), lambda i: (0, i, 0))
    return pl.pallas_call(
        functools.partial(_outproj_kernel, final=final),
        grid=(m // tm,),
        in_specs=[mspec, mspec, mspec, mspec,
                  pl.BlockSpec((D_MODEL, D_MODEL), lambda i: (0, 0)),
                  pl.BlockSpec((tm, D_MODEL), lambda i: (i, 0)),
                  pl.BlockSpec((1, D_MODEL), lambda i: (0, 0))],
        out_specs=pl.BlockSpec((tm, D_MODEL), lambda i: (i, 0)),
        out_shape=jax.ShapeDtypeStruct((m, D_MODEL), F32),
        compiler_params=_cparams("arbitrary"),
        name="outproj",
    )(mix_a, mix_b, mix_c, mix_d, w_out_bf, x2, final_w.reshape(1, D_MODEL))


def _pad_feat():
    lane = _lane()
    row = jnp.where((lane == PAD_LANE[0]) | (lane == PAD_LANE[1]), -1.0, 0.0)
    return jnp.broadcast_to(row, (TILE, LANES)).astype(BF16)


def _store_kv_aug(k, v, ka, va, pad_rows, with_blocks):
    lo = _lane_lo()
    lane = _lane()
    seq = k.shape[0]
    blk = lax.broadcasted_iota(jnp.int32, (seq, 1), 0) // SLC_BLK
    for h in range(2):
        own = lo if h == 0 else jnp.logical_not(lo)
        kfeat = jnp.where(lane - SEL_LANE[h] == blk, BIG, 0.0) if with_blocks else 0.0
        vfeat = jnp.where(lane == ONE_LANE[h], 1.0, 0.0)
        ka[h][pad_rows:pad_rows + seq, :] = jnp.where(own, k, kfeat).astype(BF16)
        va[h][pad_rows:pad_rows + seq, :] = jnp.where(own, v, vfeat).astype(BF16)
        flag = jnp.where(lane == PAD_LANE[h], BIG, 0.0)
        ka[h][0:pad_rows, :] = jnp.broadcast_to(flag, (pad_rows, LANES)).astype(BF16)
        va[h][0:pad_rows, :] = jnp.zeros((pad_rows, LANES), BF16)


def _group_both_halves(g, x):
    own = jnp.where(_lane_lo() == (g == 0), x, 0.0)
    return own + pltpu.roll(own, HEAD_DIM, axis=1)


def _attend(streams, n_chunks, kc, pad_rows, ka, va, bias_fn):
    lo = _lane_lo()
    ncb = len(streams[0][0])
    nt = kc // TILE
    qas = [(jnp.concatenate([jnp.where(lo, q, qfeat) for q in q_list], axis=0),
            jnp.concatenate([jnp.where(lo, qfeat, q) for q in q_list], axis=0))
           for q_list, qfeat, _ in streams]

    def step(k, carry):
        starts = [pl.multiple_of(pad_rows + (i + 1) * TILE - (k + 1) * kc, TILE) for _, _, i in streams]
        scores = [[_dot_nt(qa[h], ka[h][pl.ds(start, kc), :]) for h in range(2)]
                  for qa, start in zip(qas, starts)]
        bias = {}
        for h in range(2):
            for cb in range(ncb):
                for jt in range(nt):
                    bias[h, cb, jt] = bias_fn(2 * cb + h, k * nt + (nt - 1 - jt))
        probs, new_ms, alphas = [], [], []
        for si, (ms, _) in enumerate(carry):
            ms_s = [None] * (2 * ncb)
            p_s, a_s = [], []
            for h in range(2):
                s = scores[si][h]
                ps, al = [], []
                for cb in range(ncb):
                    blks = [s[cb * TILE:(cb + 1) * TILE, jt * TILE:(jt + 1) * TILE] + bias[h, cb, jt]
                            for jt in range(nt)]
                    mx = blks[0]
                    for b in blks[1:]:
                        mx = jnp.maximum(mx, b)
                    m_old = ms[2 * cb + h]
                    m_new = jnp.maximum(m_old, jnp.max(mx, axis=-1, keepdims=True))
                    al.append(jnp.broadcast_to(jnp.exp(m_old - m_new), (TILE, LANES)))
                    ps.append(jnp.concatenate([jnp.exp(b - m_new).astype(BF16) for b in blks], axis=1))
                    ms_s[2 * cb + h] = m_new
                p_s.append(jnp.concatenate(ps, axis=0))
                a_s.append(jnp.concatenate(al, axis=0))
            probs.append(p_s)
            alphas.append(a_s)
            new_ms.append(tuple(ms_s))
        out = []
        for si, (_, accs) in enumerate(carry):
            new_accs = tuple(accs[h] * alphas[si][h] + _dot(probs[si][h], va[h][pl.ds(starts[si], kc), :])
                             for h in range(2))
            out.append((new_ms[si], new_accs))
        return tuple(out)

    init = tuple((tuple(jnp.full((TILE, 1), NEG, F32) for _ in range(2 * ncb)),
                  tuple(jnp.zeros((ncb * TILE, LANES), F32) for _ in range(2))) for _ in streams)
    if n_chunks is None:
        return step(0, init)
    return lax.fori_loop(0, n_chunks, step, init)


def _split_acc(accs, cb):
    a0 = accs[0][cb * TILE:(cb + 1) * TILE, :]
    a1 = accs[1][cb * TILE:(cb + 1) * TILE, :]
    return a0, a0[:, ONE_LANE[0]:ONE_LANE[0] + 1], a1, a1[:, ONE_LANE[1]:ONE_LANE[1] + 1]


def _swa_kernel(sink_ref, q_ref, k_ref, v_ref, gate_ref, bias_ref, o_ref, ka0, ka1, va0, va1, *, seq):
    g = pl.program_id(1)
    lo = _lane_lo()
    ka, va = (ka0, ka1), (va0, va1)
    _store_kv_aug(_group_both_halves(g, k_ref[...]), _group_both_halves(g, v_ref[...]), ka, va, TILE, False)
    qfeat = _pad_feat()

    def qgroup(it, _):
        tiles = [it * BAND_STREAMS + u for u in range(BAND_STREAMS)]
        rows = [pl.ds(pl.multiple_of(i * TILE, TILE), TILE) for i in tiles]
        streams = [([(q_ref[cb, r, :] * (HEAD_DIM ** -0.5)).astype(BF16) for cb in range(2)], qfeat, i)
                   for i, r in zip(tiles, rows)]
        res = _attend(streams, None, 2 * TILE, TILE, ka, va, lambda h, d: bias_ref[h, d])
        for (ms, accs), r in zip(res, rows):
            for cb in range(2):
                a0, l0, a1, l1 = _split_acc(accs, cb)
                outs = []
                for h, (a, l) in enumerate(((a0, l0), (a1, l1))):
                    m = ms[2 * cb + h]
                    sink = sink_ref[g * 4 + 2 * cb + h]
                    mt = jnp.maximum(m, sink)
                    scale = jnp.exp(m - mt)
                    outs.append(a * (scale / (l * scale + jnp.exp(sink - mt))))
                o = jnp.where(lo, outs[0], outs[1])
                o_ref[cb, r, :] = o * _silu(gate_ref[cb, r, :])
        return 0

    lax.fori_loop(0, seq // (TILE * BAND_STREAMS), qgroup, 0)


def _swa(proj4, sinks, t_b):
    _, bsz, seq, _ = proj4.shape
    return pl.pallas_call(
        functools.partial(_swa_kernel, seq=seq),
        grid=(bsz, N_KV),
        in_specs=[
            pl.BlockSpec(memory_space=pltpu.SMEM),
            pl.BlockSpec((2, None, seq, LANES), lambda b, g: (CB_BQ // 2 + g, b, 0, 0)),
            pl.BlockSpec((None, None, seq, LANES), lambda b, g: (CB_BK, b, 0, 0)),
            pl.BlockSpec((None, None, seq, LANES), lambda b, g: (CB_BV, b, 0, 0)),
            pl.BlockSpec((2, None, seq, LANES), lambda b, g: (CB_BGATE // 2 + g, b, 0, 0)),
            pl.BlockSpec((4, 2, TILE, TILE), lambda b, g: (g, 0, 0, 0)),
        ],
        out_specs=pl.BlockSpec((2, None, seq, LANES), lambda b, g: (g, b, 0, 0)),
        out_shape=jax.ShapeDtypeStruct((4, bsz, seq, LANES), F32),
        scratch_shapes=[pltpu.VMEM((seq + TILE, LANES), BF16)] * 4,
        compiler_params=_cparams("arbitrary", "arbitrary"),
        name="swa",
    )(sinks, proj4, proj4, proj4, proj4, t_b)


def _conv_kernel(cw_ref, b_ref, c_ref, h_ref, gate_ref, o_ref, pad_ref, *, seq):
    pad_ref[0:8, :] = jnp.zeros((8, LANES), F32)
    pad_ref[8:8 + seq, :] = c_ref[...] * h_ref[...]
    w = cw_ref[...]
    y = (w[0:1, :] * pad_ref[6:6 + seq, :] + w[1:2, :] * pad_ref[7:7 + seq, :]
         + w[2:3, :] * pad_ref[8:8 + seq, :])
    o_ref[...] = b_ref[...] * y * _silu(gate_ref[...])


def _conv(proj4, conv_w):
    _, bsz, seq, _ = proj4.shape
    cw = jnp.pad(conv_w, ((0, 5), (0, 0)))

    def spec(base):
        return pl.BlockSpec((None, None, seq, LANES), lambda b, c: (base + c, b, 0, 0))

    return pl.pallas_call(
        functools.partial(_conv_kernel, seq=seq),
        grid=(bsz, 4),
        in_specs=[pl.BlockSpec((8, LANES), lambda b, c: (0, c)),
                  spec(CB_CB), spec(CB_CC), spec(CB_CH), spec(CB_CGATE)],
        out_specs=pl.BlockSpec((None, None, seq, LANES), lambda b, c: (c, b, 0, 0)),
        out_shape=jax.ShapeDtypeStruct((4, bsz, seq, LANES), F32),
        scratch_shapes=[pltpu.VMEM((seq + 8, LANES), F32)],
        compiler_params=_cparams("arbitrary", "arbitrary"),
        name="conv",
    )(cw, proj4, proj4, proj4, proj4)


def _dil1_kernel(q_ref, k_ref, v_ref, bias_ref, o_ref, lse_ref, ka0, ka1, va0, va1, *, seq):
    lo = _lane_lo()
    ka, va = (ka0, ka1), (va0, va1)
    _store_kv_aug(k_ref[...], v_ref[...], ka, va, TILE, False)
    qfeat = _pad_feat()

    def qgroup(it, _):
        tiles = [it * BAND_STREAMS + u for u in range(BAND_STREAMS)]
        rows = [pl.ds(pl.multiple_of(i * TILE, TILE), TILE) for i in tiles]
        streams = [([(q_ref[r, :] * (HEAD_DIM ** -0.5)).astype(BF16)], qfeat, i) for i, r in zip(tiles, rows)]
        res = _attend(streams, None, 2 * TILE, TILE, ka, va, lambda h, d: bias_ref[h, d])
        for (ms, accs), r in zip(res, rows):
            a0, l0, a1, l1 = _split_acc(accs, 0)
            o_ref[r, :] = jnp.where(lo, a0 / l0, a1 / l1)
            lse_ref[r, :] = jnp.where(lo, ms[0] + jnp.log(l0), ms[1] + jnp.log(l1))
        return 0

    lax.fori_loop(0, seq // (TILE * BAND_STREAMS), qgroup, 0)


def _dil1(proj4, t_d1):
    _, bsz, seq, _ = proj4.shape

    def spec(base):
        return pl.BlockSpec((None, None, seq, LANES), lambda b, h: (base + h, b, 0, 0))

    ospec = pl.BlockSpec((None, None, seq, LANES), lambda b, h: (h, b, 0, 0))
    oshape = jax.ShapeDtypeStruct((4, bsz, seq, LANES), F32)
    return pl.pallas_call(
        functools.partial(_dil1_kernel, seq=seq),
        grid=(bsz, 4),
        in_specs=[spec(CB_DQ), spec(CB_DK), spec(CB_DV),
                  pl.BlockSpec((None, 2, 2, TILE, TILE), lambda b, h: (h, 0, 0, 0, 0))],
        out_specs=(ospec, ospec),
        out_shape=(oshape, oshape),
        scratch_shapes=[pltpu.VMEM((seq + TILE, LANES), BF16)] * 4,
        compiler_params=_cparams("arbitrary", "arbitrary"),
        name="dil1",
    )(proj4, proj4, proj4, t_d1)


def _dil_kernel(q_ref, k_ref, v_ref, o1_ref, lse1_ref, gate_ref, t4_ref, t16_ref, o_ref,
                qs, kp0, kp1, vp0, vp1, o1s, l1s, gs, res, *, seq):
    lo = _lane_lo()
    na = seq // PHASES
    kp, vp = (kp0, kp1), (vp0, vp1)

    def by_phase(ref):
        return pltpu.einshape("apl->pal", ref[...].reshape(na, PHASES, LANES))

    qs[...] = (by_phase(q_ref) * (HEAD_DIM ** -0.5)).astype(BF16)
    k = by_phase(k_ref)
    kp0[...] = jnp.where(lo, k, 0.0).astype(BF16)
    kp1[...] = jnp.where(lo, 0.0, k).astype(BF16)
    v = by_phase(v_ref)
    vp0[...] = jnp.where(lo, v, 0.0).astype(BF16)
    vp1[...] = jnp.where(lo, 0.0, v).astype(BF16)
    o1s[...] = by_phase(o1_ref)
    l1s[...] = by_phase(lse1_ref)
    gs[...] = by_phase(gate_ref)

    def residue(r, _):
        phases = [4 * c + r for c in range(4)]
        q = jnp.concatenate([qs[p] for p in phases], axis=0)
        scores = [_dot_nt(q, jnp.concatenate([kp[h][p] for p in phases], axis=0)) for h in range(2)]
        e4, l4, m4, e16, l16, m16 = [], [], [], [], [], []
        for h in range(2):
            s = scores[h]
            for c in range(4):
                raw = [s[c * na:(c + 1) * na, c2 * na:(c2 + 1) * na] for c2 in range(4)]
                blks = [raw[c2] + t4_ref[h, c - c2 + 3] for c2 in range(4)]
                mx = jnp.maximum(jnp.maximum(blks[0], blks[1]), jnp.maximum(blks[2], blks[3]))
                m = jnp.max(mx, axis=-1, keepdims=True)
                eb = [jnp.exp(b - m) for b in blks]
                l4.append(jnp.sum(eb[0] + eb[1] + eb[2] + eb[3], axis=-1, keepdims=True))
                m4.append(m)
                e4.append(jnp.concatenate([e.astype(BF16) for e in eb], axis=1))
                s16 = raw[c] + t16_ref[h]
                m = jnp.max(s16, axis=-1, keepdims=True)
                e = jnp.exp(s16 - m)
                l16.append(jnp.sum(e, axis=-1, keepdims=True))
                m16.append(m)
                e16.append(e.astype(BF16))
        pv4 = [_dot(jnp.concatenate(e4[4 * h:4 * h + 4], axis=0),
                    jnp.concatenate([vp[h][p] for p in phases], axis=0)) for h in range(2)]
        pv16 = [[_dot(e16[4 * h + c], vp[h][phases[c]]) for c in range(4)] for h in range(2)]
        for c, p in enumerate(phases):
            o4 = pv4[0][c * na:(c + 1) * na, :] / l4[c] + pv4[1][c * na:(c + 1) * na, :] / l4[4 + c]
            o16 = pv16[0][c] / l16[c] + pv16[1][c] / l16[4 + c]
            lb = jnp.where(lo, m4[c] + jnp.log(l4[c]), m4[4 + c] + jnp.log(l4[4 + c]))
            lc = jnp.where(lo, m16[c] + jnp.log(l16[c]), m16[4 + c] + jnp.log(l16[4 + c]))
            la = l1s[p]
            mx = jnp.maximum(jnp.maximum(la, lb), lc)
            ea, eb, ec = jnp.exp(la - mx), jnp.exp(lb - mx), jnp.exp(lc - mx)
            o = (ea * o1s[p] + eb * o4 + ec * o16) / (ea + eb + ec)
            res[p] = o * _silu(gs[p])
        return 0

    lax.fori_loop(0, 4, residue, 0)
    o_ref[...] = pltpu.einshape("pal->apl", res[...]).reshape(seq, LANES)


def _dil(proj4, o1, lse1, t_d4, t_d16):
    _, bsz, seq, _ = proj4.shape
    na = seq // PHASES

    def spec(base):
        return pl.BlockSpec((None, None, seq, LANES), lambda b, h: (base + h, b, 0, 0))

    return pl.pallas_call(
        functools.partial(_dil_kernel, seq=seq),
        grid=(bsz, 4),
        in_specs=[spec(CB_DQ), spec(CB_DK), spec(CB_DV), spec(0), spec(0), spec(CB_DGATE),
                  pl.BlockSpec((None, 2, 7, TILE, TILE), lambda b, h: (h, 0, 0, 0, 0)),
                  pl.BlockSpec((None, 2, TILE, TILE), lambda b, h: (h, 0, 0, 0))],
        out_specs=spec(0),
        out_shape=jax.ShapeDtypeStruct((4, bsz, seq, LANES), F32),
        scratch_shapes=[pltpu.VMEM((PHASES, na, LANES), BF16)] * 5 + [pltpu.VMEM((PHASES, na, LANES), F32)] * 4,
        compiler_params=_cparams("arbitrary", "arbitrary"),
        name="dil",
    )(proj4, proj4, proj4, o1, lse1, proj4, t_d4, t_d16)


def _compress_kernel(kv_ref, pos_ref, wlo_ref, whi_ref, w2_ref, o_ref, *, seq):
    na = seq // PHASES
    ph = pltpu.einshape("apl->pal", kv_ref[...].reshape(na, PHASES, LANES))
    r = jnp.concatenate([ph[p] for p in range(PHASES)], axis=1)
    h_lo = _dot((r + pos_ref[0:1, :]).astype(BF16), wlo_ref[...])
    h_hi = _dot((r + pos_ref[1:2, :]).astype(BF16), whi_ref[...])
    h = h_lo + pltpu.roll(h_hi, na - 1, axis=0)
    o_ref[...] = _dot(_silu(h).astype(BF16), w2_ref[...])


def _compress(proj4, pos2, w_lo, w_hi, w2):
    _, bsz, seq, _ = proj4.shape
    na = seq // PHASES
    wide = PHASES * LANES
    full = lambda shape: pl.BlockSpec(shape, lambda b, g: (0,) * len(shape))
    return pl.pallas_call(
        functools.partial(_compress_kernel, seq=seq),
        grid=(bsz, N_KV),
        in_specs=[pl.BlockSpec((None, None, seq, LANES), lambda b, g: (CB_ACMP + g, b, 0, 0)),
                  full((2, wide)), full((wide, 2 * CMP_HIDDEN)), full((wide, 2 * CMP_HIDDEN)),
                  full((2 * CMP_HIDDEN, LANES))],
        out_specs=pl.BlockSpec((None, None, na, LANES), lambda b, g: (b, g, 0, 0)),
        out_shape=jax.ShapeDtypeStruct((bsz, N_KV, na, LANES), F32),
        compiler_params=_cparams("arbitrary", "arbitrary"),
        name="compress",
    )(proj4, pos2, w_lo, w_hi, w2)


def _cmp_kernel(q_ref, kv_ref, bias_ref, ovt_ref, o_ref, feat_ref, *, seq):
    lo = _lane_lo()
    kv = kv_ref[...]
    k_lo = jnp.where(lo, kv, 0.0)
    v_hi = jnp.where(lo, 0.0, kv)
    kp = (k_lo.astype(BF16), pltpu.roll(k_lo, HEAD_DIM, axis=1).astype(BF16))
    vp = (pltpu.roll(v_hi, HEAD_DIM, axis=1).astype(BF16), v_hi.astype(BF16))
    ovt = ovt_ref[...]
    n_slc = ovt.shape[0]
    first = lax.broadcasted_iota(jnp.int32, (n_slc, CMP_ROWS), 0) == 0
    pad_rows = jnp.where(first, -1.0, 0.0).astype(BF16)
    eye = (lax.broadcasted_iota(jnp.int32, (TILE, TILE), 0)
           == lax.broadcasted_iota(jnp.int32, (TILE, TILE), 1)).astype(BF16)

    def qgroup(it, _):
        r0 = pl.multiple_of(it * CMP_ROWS, CMP_ROWS)
        rows = pl.ds(r0, CMP_ROWS)
        qs = [(q_ref[cb, rows, :] * (HEAD_DIM ** -0.5)).astype(BF16) for cb in range(2)]
        scores = [[_dot_nt(qs[cb], kp[hh]) for hh in range(2)] for cb in range(2)]
        probs = [[None, None], [None, None]]
        psum = jnp.zeros((CMP_ROWS, LANES), F32)
        for cb in range(2):
            for hh in range(2):
                s = scores[cb][hh] + bias_ref[2 * cb + hh, rows, :]
                m = jnp.maximum(jnp.max(s, axis=-1, keepdims=True), -1e20)
                e = jnp.exp(s - m)
                p = e / jnp.maximum(jnp.sum(e, axis=-1, keepdims=True), 1e-30)
                psum = psum + p
                probs[cb][hh] = p.astype(BF16)
        for cb in range(2):
            o_ref[cb, rows, :] = _dot(probs[cb][0], vp[0]) + _dot(probs[cb][1], vp[1])
        p1 = psum.astype(BF16)
        rem = psum - p1.astype(F32)
        p2 = rem.astype(BF16)
        p3 = (rem - p2.astype(F32)).astype(BF16)
        imp = _dot_nt(ovt, p1) + _dot_nt(ovt, p2) + _dot_nt(ovt, p3)
        blk = lax.broadcasted_iota(jnp.int32, (n_slc, CMP_ROWS), 0)
        t = r0 + lax.broadcasted_iota(jnp.int32, (n_slc, CMP_ROWS), 1)
        cur = t // SLC_BLK
        forced = (blk == 0) | (blk == cur) | (blk == cur - 1)
        imp = jnp.where(blk > cur, NEG, jnp.where(forced, FORCE, imp))
        rank = jnp.zeros((n_slc, CMP_ROWS), F32)
        for mth in range(n_slc):
            row = imp[mth:mth + 1, :]
            tie = (blk > mth).astype(F32)
            rank = rank + jnp.where(row > imp, 1.0, jnp.where(row == imp, tie, 0.0))
        unsel = jnp.where(rank < N_SEL, 0.0, -1.0).astype(BF16)
        half = jnp.concatenate([unsel, pad_rows], axis=0)
        feat_t = jnp.concatenate([half, half], axis=0)
        for u in range(CMP_ROWS // TILE):
            piece = _dot_nt(eye, feat_t[:, u * TILE:(u + 1) * TILE])
            feat_ref[pl.ds(r0 + u * TILE, TILE), :] = piece.astype(BF16)
        return 0

    lax.fori_loop(0, seq // CMP_ROWS, qgroup, 0)


def _cmp_select(proj4, kvcmp, t_cmp, ovt):
    _, bsz, seq, _ = proj4.shape
    return pl.pallas_call(
        functools.partial(_cmp_kernel, seq=seq),
        grid=(bsz, N_KV),
        in_specs=[pl.BlockSpec((2, None, seq, LANES), lambda b, g: (CB_AQ // 2 + g, b, 0, 0)),
                  pl.BlockSpec((None, None, TILE, LANES), lambda b, g: (b, g, 0, 0)),
                  pl.BlockSpec((4, seq, LANES), lambda b, g: (g, 0, 0)),
                  pl.BlockSpec(ovt.shape, lambda b, g: (0, 0))],
        out_specs=(pl.BlockSpec((2, None, seq, LANES), lambda b, g: (g, b, 0, 0)),
                   pl.BlockSpec((None, None, seq, LANES), lambda b, g: (b, g, 0, 0))),
        out_shape=(jax.ShapeDtypeStruct((4, bsz, seq, LANES), F32),
                   jax.ShapeDtypeStruct((bsz, N_KV, seq, LANES), BF16)),
        compiler_params=_cparams("arbitrary", "arbitrary"),
        name="cmp_select",
    )(proj4, kvcmp, t_cmp, ovt)


def _slcwin_kernel(q_ref, ks_ref, vs_ref, kw_ref, vw_ref, ocmp_ref, feat_ref, gates_ref, agate_ref,
                   tslc_ref, twin_ref, x_ref, o_ref,
                   ksa0, ksa1, vsa0, vsa1, kwa0, kwa1, vwa0, vwa1, *, seq):
    g = pl.program_id(1)
    lo = _lane_lo()
    ksa, vsa, kwa, vwa = (ksa0, ksa1), (vsa0, vsa1), (kwa0, kwa1), (vwa0, vwa1)
    slc_pad = SLC_CHUNK - TILE
    win_pad = (WIN_TILES - 1) * TILE
    _store_kv_aug(_group_both_halves(g, ks_ref[...]), _group_both_halves(g, vs_ref[...]),
                  ksa, vsa, slc_pad, True)
    _store_kv_aug(_group_both_halves(g, kw_ref[...]), _group_both_halves(g, vw_ref[...]),
                  kwa, vwa, win_pad, False)
    slc_tiles = SLC_CHUNK // TILE

    def qgroup(it, _):
        tiles = [it * SLC_STREAMS + u for u in range(SLC_STREAMS)]
        rows = [pl.ds(pl.multiple_of(i * TILE, TILE), TILE) for i in tiles]
        streams = [([(q_ref[cb, r, :] * (HEAD_DIM ** -0.5)).astype(BF16) for cb in range(2)],
                    feat_ref[r, :], i) for i, r in zip(tiles, rows)]
        n_chunks = (tiles[0] + slc_tiles) // slc_tiles
        res_w = _attend(streams, None, WIN_TILES * TILE, win_pad, kwa, vwa, lambda h, d: twin_ref[h, d])
        res_s = _attend(streams, n_chunks, SLC_CHUNK, slc_pad, ksa, vsa, lambda h, d: tslc_ref[h, d])
        for (_, acc_s), (_, acc_w), r in zip(res_s, res_w, rows):
            sg = jax.nn.sigmoid(gates_ref[r, :])
            sg1 = sg.astype(BF16)
            sg2 = (sg - sg1.astype(F32)).astype(BF16)
            for cb in range(2):
                a0, l0, a1, l1 = _split_acc(acc_s, cb)
                o_slc = jnp.where(lo, a0 / l0, a1 / l1)
                a0, l0, a1, l1 = _split_acc(acc_w, cb)
                o_win = jnp.where(lo, a0 / l0, a1 / l1)
                gts = [_dot(sg1, x_ref[br, cb]) + _dot(sg2, x_ref[br, cb]) for br in range(3)]
                o = gts[0] * ocmp_ref[cb, r, :] + gts[1] * o_slc + gts[2] * o_win
                o_ref[cb, r, :] = o * _silu(agate_ref[cb, r, :])
        return 0

    assert slc_tiles % SLC_STREAMS == 0
    lax.fori_loop(0, seq // (TILE * SLC_STREAMS), qgroup, 0)


def _slcwin(proj4, o_cmp, feat, t_slc, t_win, x_tab):
    _, bsz, seq, _ = proj4.shape
    nt = seq // TILE

    def one(cb):
        return pl.BlockSpec((None, None, seq, LANES), lambda b, g: (cb, b, 0, 0))

    def two(base):
        return pl.BlockSpec((2, None, seq, LANES), lambda b, g: (base // 2 + g, b, 0, 0))

    slc_rows = seq + SLC_CHUNK - TILE
    win_rows = seq + (WIN_TILES - 1) * TILE
    return pl.pallas_call(
        functools.partial(_slcwin_kernel, seq=seq),
        grid=(bsz, N_KV),
        in_specs=[two(CB_AQ), one(CB_AKS), one(CB_AVS), one(CB_AKW), one(CB_AVW),
                  pl.BlockSpec((2, None, seq, LANES), lambda b, g: (g, b, 0, 0)),
                  pl.BlockSpec((None, None, seq, LANES), lambda b, g: (b, g, 0, 0)),
                  one(CB_AGATES), two(CB_AGATE),
                  pl.BlockSpec((4, nt, TILE, TILE), lambda b, g: (g, 0, 0, 0)),
                  pl.BlockSpec((4, WIN_TILES, TILE, TILE), lambda b, g: (g, 0, 0, 0)),
                  pl.BlockSpec((None, 3, 2, TILE, TILE), lambda b, g: (g, 0, 0, 0, 0))],
        out_specs=pl.BlockSpec((2, None, seq, LANES), lambda b, g: (g, b, 0, 0)),
        out_shape=jax.ShapeDtypeStruct((4, bsz, seq, LANES), F32),
        scratch_shapes=[pltpu.VMEM((slc_rows, LANES), BF16)] * 4 + [pltpu.VMEM((win_rows, LANES), BF16)] * 4,
        compiler_params=_cparams("arbitrary", "arbitrary"),
        name="slcwin",
    )(proj4, proj4, proj4, proj4, proj4, o_cmp, feat, proj4, proj4, t_slc, t_win, x_tab)


def _toeplitz_kernel(par_ref, v_ref, o_ref, *, n_tiles, interleave):
    a = lax.broadcasted_iota(jnp.int32, (TILE, TILE), 0)
    b = lax.broadcasted_iota(jnp.int32, (TILE, TILE), 1)
    for t in range(n_tiles):
        x = jnp.broadcast_to(v_ref[t:t + 1, :], (TILE, 2 * TILE))
        y = pltpu.roll(x, 0, 1, stride=1, stride_axis=0)[:, :TILE]
        dist = par_ref[t, 0] + par_ref[t, 1] * (a - b)
        ok = (dist >= par_ref[t, 2]) & (dist <= par_ref[t, 3]) & (b < par_ref[t, 4])
        tile = jnp.where(ok, y, NEG)
        if interleave:
            o_ref[pl.ds(t, TILE, stride=n_tiles), :] = tile
        else:
            o_ref[t] = tile


def _toeplitz(v, params, interleave=False):
    heads, n_tiles, _ = v.shape
    if interleave:
        oshape, ospec = (heads, n_tiles * TILE, TILE), pl.BlockSpec((None, n_tiles * TILE, TILE), lambda h: (h, 0, 0))
    else:
        oshape, ospec = (heads, n_tiles, TILE, TILE), pl.BlockSpec((None, n_tiles, TILE, TILE), lambda h: (h, 0, 0, 0))
    return pl.pallas_call(
        functools.partial(_toeplitz_kernel, n_tiles=n_tiles, interleave=interleave),
        grid=(heads,),
        in_specs=[pl.BlockSpec(memory_space=pltpu.SMEM),
                  pl.BlockSpec((None, n_tiles, 2 * TILE), lambda h: (h, 0, 0))],
        out_specs=ospec,
        out_shape=jax.ShapeDtypeStruct(oshape, F32),
        compiler_params=_cparams("arbitrary"),
        name="toeplitz",
    )(jnp.asarray(params, jnp.int32), v)


def _t5_bucket(d):
    exact = NUM_BUCKETS // 2
    large = exact + (jnp.log(jnp.maximum(d, exact).astype(F32) / exact)
                     / math.log(MAX_DISTANCE / exact) * (NUM_BUCKETS - exact)).astype(jnp.int32)
    return jnp.where(d < exact, d, jnp.minimum(large, NUM_BUCKETS - 1))


def _generator_rows(ext, off, base, mul):
    lo = off + base - 127 * mul
    first = ext[:, lo:lo + 127 * mul + 1:mul][:, ::-1]
    second = ext[:, off + base + mul:off + base + 128 * mul + 1:mul][:, ::-1]
    return jnp.concatenate([first, second], axis=1)


def _bias_tables(rel_bias, seq):
    big = 1 << 30
    onehot = jax.nn.one_hot(_t5_bucket(jnp.arange(seq, dtype=jnp.int32)), NUM_BUCKETS, dtype=F32)
    bias_d = jnp.dot(onehot, rel_bias, precision=lax.Precision.HIGHEST).T
    off = 128 * PHASES + 64
    ext = jnp.pad(bias_d, ((0, 0), (off, off)))
    ba, bb, bd = ext[:8], ext[8:16], ext[16:24]
    nt = seq // TILE

    def plain(e, n, hi):
        v = jnp.stack([_generator_rows(e, off, TILE * d, 1) for d in range(n)], axis=1)
        return _toeplitz(v, [(TILE * d, 1, 0, hi, TILE) for d in range(n)])

    t_slc = plain(ba, nt, big)
    t_win = plain(ba, WIN_TILES, WIN_A - 1)
    t_b = plain(bb, 2, WIN_B - 1)
    t_d1 = plain(bd, 2, DIL_MAXDIST)
    v4 = jnp.stack([_generator_rows(bd, off, 4 * dl, 16) for dl in range(-3, 4)], axis=1)
    t_d4 = _toeplitz(v4, [(dl, 4, 0, DIL_MAXDIST, TILE) for dl in range(-3, 4)])
    v16 = jnp.stack([_generator_rows(bd, off, 0, 16)], axis=1)
    t_d16 = _toeplitz(v16, [(0, 1, 0, big, TILE)])
    n_cmp = (seq - CMP_BLK) // CMP_STRIDE + 1
    vc = jnp.stack([_generator_rows(ba, off, p - (CMP_BLK - 1), CMP_STRIDE) for p in range(PHASES)], axis=1)
    t_cmp = _toeplitz(vc, [(p - (CMP_BLK - 1), CMP_STRIDE, 0, big, n_cmp) for p in range(PHASES)],
                      interleave=True)
    return dict(slc=t_slc, win=t_win, b=t_b, d1=t_d1.reshape(4, 2, 2, TILE, TILE),
                d4=t_d4.reshape(4, 2, 7, TILE, TILE), d16=t_d16.reshape(4, 2, TILE, TILE), cmp=t_cmp)


def _const_tables(seq):
    n_cmp = (seq - CMP_BLK) // CMP_STRIDE + 1
    n_slc = seq // SLC_BLK
    c0 = np.arange(TILE)[None, :] * CMP_STRIDE
    s0 = np.arange(n_slc)[:, None] * SLC_BLK
    ovt = ((c0 < s0 + SLC_BLK) & (c0 + CMP_BLK > s0) & (np.arange(TILE)[None, :] < n_cmp))
    row = np.arange(TILE)[None, None, None, :, None]
    lane = np.arange(TILE)[None, None, None, None, :]
    g = np.arange(N_KV)[:, None, None, None, None]
    br = np.arange(3)[None, :, None, None, None]
    cb = np.arange(2)[None, None, :, None, None]
    x_tab = (row == br * N_HEADS + 4 * g + 2 * cb + lane // HEAD_DIM)
    as_bf = lambda m: jnp.asarray(m.astype(np.float32), BF16)
    return as_bf(ovt), as_bf(x_tab)


def _permute_w_in(w):
    pad = jnp.zeros((w.shape[0], N_CB * LANES - D_IN), w.dtype)
    return jnp.concatenate([
        w[:, 0:512],
        w[:, 512:576], w[:, 640:704], w[:, 576:640], w[:, 704:768],
        w[:, 768:1280],
        w[:, 1304:D_IN],
        w[:, 1280:1304], pad], axis=1).astype(BF16)


def _compress_weights(cmp_pos, cmp_w1, cmp_w2):
    half = CMP_BLK // 2
    w1 = cmp_w1.reshape(2, CMP_BLK, HEAD_DIM, CMP_HIDDEN)
    z = jnp.zeros((half, HEAD_DIM, CMP_HIDDEN), F32)

    def stack(part):
        top = jnp.concatenate([w1[0, part], z], axis=-1)
        bot = jnp.concatenate([z, w1[1, part]], axis=-1)
        return jnp.concatenate([top, bot], axis=1).reshape(half * LANES, 2 * CMP_HIDDEN).astype(BF16)

    w_lo, w_hi = stack(slice(0, half)), stack(slice(half, CMP_BLK))
    z2 = jnp.zeros((CMP_HIDDEN, HEAD_DIM), F32)
    w2 = jnp.concatenate([jnp.concatenate([cmp_w2[0], z2], axis=1),
                          jnp.concatenate([z2, cmp_w2[1]], axis=1)], axis=0).astype(BF16)
    pos = jnp.concatenate([cmp_pos[0], cmp_pos[1]], axis=-1)
    pos2 = pos.reshape(2, half * LANES)
    return pos2, w_lo, w_hi, w2


def _layer(x2, bsz, seq, norm_w, w_in, w_out, conv_w, sinks, cmp_pos, cmp_w1, cmp_w2, final_w, final,
           tabs, consts):
    ovt, x_tab = consts
    proj = _inproj(x2, norm_w, _permute_w_in(w_in))
    proj4 = proj.reshape(N_CB, bsz, seq, LANES)
    kvcmp = _compress(proj4, *_compress_weights(cmp_pos, cmp_w1, cmp_w2))
    o_cmp, feat = _cmp_select(proj4, kvcmp, tabs["cmp"], ovt)
    mix_a = _slcwin(proj4, o_cmp, feat, tabs["slc"], tabs["win"], x_tab)
    mix_b = _swa(proj4, sinks, tabs["b"])
    mix_c = _conv(proj4, conv_w)
    o1, lse1 = _dil1(proj4, tabs["d1"])
    mix_d = _dil(proj4, o1, lse1, tabs["d4"], tabs["d16"])
    flat = lambda t: t.reshape(4, bsz * seq, LANES)
    return _outproj(flat(mix_a), flat(mix_b), flat(mix_c), flat(mix_d), w_out.astype(BF16), x2,
                    final_w, final)


def kernel(x, norm_w, w_in, w_out, conv_w, sinks, cmp_pos, cmp_w1, cmp_w2, rel_bias, final_norm_w):
    bsz, seq, _ = x.shape
    depth = norm_w.shape[0]
    tabs = _bias_tables(rel_bias, seq)
    consts = _const_tables(seq)
    x2 = x.reshape(bsz * seq, D_MODEL)
    for layer in range(depth):
        x2 = _layer(x2, bsz, seq, norm_w[layer], w_in[layer], w_out[layer], conv_w[layer], sinks[layer],
                    cmp_pos[layer], cmp_w1[layer], cmp_w2[layer], final_norm_w, layer == depth - 1,
                    tabs, consts)
    return x2.reshape(bsz, seq, D_MODEL)
```

```python
import functools
import math

import numpy as np
import jax
import jax.numpy as jnp
from jax import lax
from jax.experimental import pallas as pl
from jax.experimental.pallas import tpu as pltpu

F32 = jnp.float32
BF16 = jnp.bfloat16

D_MODEL = 2048
HEAD_DIM = 64
N_HEADS = 8
N_KV = 2
CMP_BLK = 32
CMP_STRIDE = 16
CMP_HIDDEN = 128
SLC_BLK = 64
N_SEL = 8
WIN_A = 512
WIN_B = 128
DIL_MAXDIST = 128
NUM_BUCKETS = 32
MAX_DISTANCE = 2048
RMS_EPS = 1e-6
NEG = -1e30
BIG = 1e30
FORCE = 1e4
D_IN = 7192
PHASES = 16
LOG2E = math.log2(math.e)
Q_SCALE = HEAD_DIM ** -0.5 * LOG2E

LANES = 128
TILE = 128
VMEM_LIMIT = 56 * 1024 * 1024

SEL_LANE = (HEAD_DIM, 0)
PAD_LANE = (HEAD_DIM + 32, 32)
ONE_LANE = (HEAD_DIM, 0)

CB_AQ = 0
CB_ACMP = 4
CB_AKS, CB_AVS, CB_AKW, CB_AVW = 6, 7, 8, 9
CB_AGATE = 10
CB_BQ, CB_BK, CB_BV, CB_BGATE = 14, 18, 19, 20
CB_CB, CB_CC, CB_CH, CB_CGATE = 24, 28, 32, 36
CB_DQ, CB_DK, CB_DV, CB_DGATE = 40, 44, 48, 52
CB_AGATES = 56
N_CB = 57
CB_PER_STEP = 19

SLC_CHUNK = 4 * TILE
WIN_TILES = 5
SWA_STREAMS = 4
DIL1_STREAMS = 8
SLC_STREAMS = 4
CMP_ROWS = 8 * TILE


def _cparams(*sem):
    return pltpu.CompilerParams(dimension_semantics=sem, vmem_limit_bytes=VMEM_LIMIT)


def _lane():
    return lax.broadcasted_iota(jnp.int32, (1, LANES), 1)


def _lane_lo():
    return _lane() < HEAD_DIM


def _dot_nt(a, b):
    return lax.dot_general(a, b, (((1,), (1,)), ((), ())), preferred_element_type=F32)


def _dot(a, b):
    return jnp.dot(a, b, preferred_element_type=F32)


def _silu(x):
    return x * jax.nn.sigmoid(x)


def _inproj_kernel(x_ref, nw_ref, w_ref, o_ref):
    x = x_ref[...]
    y = x * lax.rsqrt(jnp.mean(x * x, axis=-1, keepdims=True) + RMS_EPS)
    xn = (y * nw_ref[...]).astype(BF16)
    acc = _dot(xn, w_ref[...])
    for k in range(CB_PER_STEP):
        o_ref[k] = acc[:, k * LANES:(k + 1) * LANES]


def _inproj(x2, norm_w, w_perm, layer, tm=512):
    m = x2.shape[0]
    n_steps = N_CB // CB_PER_STEP
    return pl.pallas_call(
        _inproj_kernel,
        grid=(n_steps, m // tm),
        in_specs=[
            pl.BlockSpec((tm, D_MODEL), lambda j, i: (i, 0)),
            pl.BlockSpec((1, D_MODEL), lambda j, i: (0, 0)),
            pl.BlockSpec((None, D_MODEL, CB_PER_STEP * LANES), lambda j, i: (layer, 0, j)),
        ],
        out_specs=pl.BlockSpec((CB_PER_STEP, tm, LANES), lambda j, i: (j, i, 0)),
        out_shape=jax.ShapeDtypeStruct((N_CB, m, LANES), F32),
        compiler_params=_cparams("arbitrary", "arbitrary"),
        name="inproj",
    )(x2, norm_w.reshape(1, D_MODEL), w_perm)


def _outproj_kernel(a_ref, b_ref, c_ref, d_ref, w_ref, x_ref, fw_ref, o_ref, *, final):
    mix = jnp.concatenate([r[k] for r in (a_ref, b_ref, c_ref, d_ref) for k in range(4)], axis=1)
    y = x_ref[...] + _dot(mix.astype(BF16), w_ref[...])
    if final:
        y = y * lax.rsqrt(jnp.mean(y * y, axis=-1, keepdims=True) + RMS_EPS) * fw_ref[...]
    o_ref[...] = y


def _outproj(mix_a, mix_b, mix_c, mix_d, w_out_bf, x2, final_w, final, tm=512):
    m = x2.shape[0]
    mspec = pl.BlockSpec((4, tm, LANES), lambda i: (0, i, 0))
    return pl.pallas_call(
        functools.partial(_outproj_kernel, final=final),
        grid=(m // tm,),
        in_specs=[mspec, mspec, mspec, mspec,
                  pl.BlockSpec((D_MODEL, D_MODEL), lambda i: (0, 0)),
                  pl.BlockSpec((tm, D_MODEL), lambda i: (i, 0)),
                  pl.BlockSpec((1, D_MODEL), lambda i: (0, 0))],
        out_specs=pl.BlockSpec((tm, D_MODEL), lambda i: (i, 0)),
        out_shape=jax.ShapeDtypeStruct((m, D_MODEL), F32),
        compiler_params=_cparams("arbitrary"),
        name="outproj",
    )(mix_a, mix_b, mix_c, mix_d, w_out_bf, x2, final_w.reshape(1, D_MODEL))


def _pad_feat():
    lane = _lane()
    row = jnp.where((lane == PAD_LANE[0]) | (lane == PAD_LANE[1]), -1.0, 0.0)
    return jnp.broadcast_to(row, (TILE, LANES)).astype(BF16)


def _store_kv_aug(k, v, ka, va, pad_rows, with_blocks):
    lo = _lane_lo()
    lane = _lane()
    seq = k.shape[0]
    blk = lax.broadcasted_iota(jnp.int32, (seq, 1), 0) // SLC_BLK
    for h in range(2):
        own = lo if h == 0 else jnp.logical_not(lo)
        kfeat = jnp.where(lane - SEL_LANE[h] == blk, BIG, 0.0) if with_blocks else 0.0
        vfeat = jnp.where(lane == ONE_LANE[h], 1.0, 0.0)
        ka[h][pad_rows:pad_rows + seq, :] = jnp.where(own, k, kfeat).astype(BF16)
        va[h][pad_rows:pad_rows + seq, :] = jnp.where(own, v, vfeat).astype(BF16)
        flag = jnp.where(lane == PAD_LANE[h], BIG, 0.0)
        ka[h][0:pad_rows, :] = jnp.broadcast_to(flag, (pad_rows, LANES)).astype(BF16)
        va[h][0:pad_rows, :] = jnp.zeros((pad_rows, LANES), BF16)


def _group_both_halves(g, x):
    own = jnp.where(_lane_lo() == (g == 0), x, 0.0)
    return own + pltpu.roll(own, HEAD_DIM, axis=1)


def _attend(streams, n_chunks, kc, pad_rows, ka, va, bias_fn):
    lo = _lane_lo()
    ncb = len(streams[0][0])
    nt = kc // TILE
    qas = [(jnp.concatenate([jnp.where(lo, q, qfeat) for q in q_list], axis=0),
            jnp.concatenate([jnp.where(lo, qfeat, q) for q in q_list], axis=0))
           for q_list, qfeat, _ in streams]

    def chunk_starts(k):
        return [pl.multiple_of(pad_rows + (i + 1) * TILE - (k + 1) * kc, TILE) for _, _, i in streams]

    def chunk_scores(k):
        return tuple(tuple(_dot_nt(qa[h], ka[h][pl.ds(start, kc), :]) for h in range(2))
                     for qa, start in zip(qas, chunk_starts(k)))

    def step(k, scores, carry):
        starts = chunk_starts(k)
        bias = {}
        for h in range(2):
            for cb in range(ncb):
                for jt in range(nt):
                    bias[h, cb, jt] = bias_fn(2 * cb + h, k * nt + (nt - 1 - jt))
        probs, new_ms, alphas = [], [], []
        for si, (ms, _) in enumerate(carry):
            ms_s = [None] * (2 * ncb)
            p_s, a_s = [], []
            for h in range(2):
                s = scores[si][h]
                ps, al = [], []
                for cb in range(ncb):
                    blks = [s[cb * TILE:(cb + 1) * TILE, jt * TILE:(jt + 1) * TILE] + bias[h, cb, jt]
                            for jt in range(nt)]
                    mx = blks[0]
                    for b in blks[1:]:
                        mx = jnp.maximum(mx, b)
                    m_old = ms[2 * cb + h]
                    m_new = jnp.maximum(m_old, jnp.max(mx, axis=-1, keepdims=True))
                    al.append(jnp.broadcast_to(jnp.exp2(m_old - m_new), (TILE, LANES)))
                    ps.append(jnp.concatenate([jnp.exp2(b - m_new).astype(BF16) for b in blks], axis=1))
                    ms_s[2 * cb + h] = m_new
                p_s.append(jnp.concatenate(ps, axis=0))
                a_s.append(jnp.concatenate(al, axis=0))
            probs.append(p_s)
            alphas.append(a_s)
            new_ms.append(tuple(ms_s))
        out = []
        for si, (_, accs) in enumerate(carry):
            new_accs = tuple(accs[h] * alphas[si][h] + _dot(probs[si][h], va[h][pl.ds(starts[si], kc), :])
                             for h in range(2))
            out.append((new_ms[si], new_accs))
        return tuple(out)

    init = tuple((tuple(jnp.full((TILE, 1), NEG, F32) for _ in range(2 * ncb)),
                  tuple(jnp.zeros((ncb * TILE, LANES), F32) for _ in range(2))) for _ in streams)
    if n_chunks is None:
        return step(0, chunk_scores(0), init)

    return lax.fori_loop(0, n_chunks, lambda k, carry: step(k, chunk_scores(k), carry), init)


def _split_acc(accs, cb):
    a0 = accs[0][cb * TILE:(cb + 1) * TILE, :]
    a1 = accs[1][cb * TILE:(cb + 1) * TILE, :]
    return a0, a0[:, ONE_LANE[0]:ONE_LANE[0] + 1], a1, a1[:, ONE_LANE[1]:ONE_LANE[1] + 1]


def _swa_kernel(sink_ref, q_ref, k_ref, v_ref, gate_ref, bias_ref, o_ref, ka0, ka1, va0, va1, *, seq):
    g = pl.program_id(1)
    lo = _lane_lo()
    ka, va = (ka0, ka1), (va0, va1)
    _store_kv_aug(_group_both_halves(g, k_ref[...]), _group_both_halves(g, v_ref[...]), ka, va, TILE, False)
    qfeat = _pad_feat()

    def qgroup(it, _):
        tiles = [it * SWA_STREAMS + u for u in range(SWA_STREAMS)]
        rows = [pl.ds(pl.multiple_of(i * TILE, TILE), TILE) for i in tiles]
        streams = [([(q_ref[cb, r, :] * Q_SCALE).astype(BF16) for cb in range(2)], qfeat, i)
                   for i, r in zip(tiles, rows)]
        res = _attend(streams, None, 2 * TILE, TILE, ka, va, lambda h, d: bias_ref[h, d])
        for (ms, accs), r in zip(res, rows):
            for cb in range(2):
                a0, l0, a1, l1 = _split_acc(accs, cb)
                outs = []
                for h, (a, l) in enumerate(((a0, l0), (a1, l1))):
                    m = ms[2 * cb + h]
                    sink = sink_ref[g * 4 + 2 * cb + h] * LOG2E
                    mt = jnp.maximum(m, sink)
                    scale = jnp.exp2(m - mt)
                    outs.append(a * (scale / (l * scale + jnp.exp2(sink - mt))))
                o = jnp.where(lo, outs[0], outs[1])
                o_ref[cb, r, :] = o * _silu(gate_ref[cb, r, :])
        return 0

    lax.fori_loop(0, seq // (TILE * SWA_STREAMS), qgroup, 0)


def _swa(proj4, sinks, t_b):
    _, bsz, seq, _ = proj4.shape
    return pl.pallas_call(
        functools.partial(_swa_kernel, seq=seq),
        grid=(bsz, N_KV),
        in_specs=[
            pl.BlockSpec(memory_space=pltpu.SMEM),
            pl.BlockSpec((2, None, seq, LANES), lambda b, g: (CB_BQ // 2 + g, b, 0, 0)),
            pl.BlockSpec((None, None, seq, LANES), lambda b, g: (CB_BK, b, 0, 0)),
            pl.BlockSpec((None, None, seq, LANES), lambda b, g: (CB_BV, b, 0, 0)),
            pl.BlockSpec((2, None, seq, LANES), lambda b, g: (CB_BGATE // 2 + g, b, 0, 0)),
            pl.BlockSpec((4, 2, TILE, TILE), lambda b, g: (g, 0, 0, 0)),
        ],
        out_specs=pl.BlockSpec((2, None, seq, LANES), lambda b, g: (g, b, 0, 0)),
        out_shape=jax.ShapeDtypeStruct((4, bsz, seq, LANES), F32),
        scratch_shapes=[pltpu.VMEM((seq + TILE, LANES), BF16)] * 4,
        compiler_params=_cparams("arbitrary", "arbitrary"),
        name="swa",
    )(sinks, proj4, proj4, proj4, proj4, t_b)


def _conv_kernel(cw_ref, b_ref, c_ref, h_ref, gate_ref, o_ref, pad_ref, *, seq):
    pad_ref[0:8, :] = jnp.zeros((8, LANES), F32)
    pad_ref[8:8 + seq, :] = c_ref[...] * h_ref[...]
    w = cw_ref[...]
    y = (w[0:1, :] * pad_ref[6:6 + seq, :] + w[1:2, :] * pad_ref[7:7 + seq, :]
         + w[2:3, :] * pad_ref[8:8 + seq, :])
    o_ref[...] = b_ref[...] * y * _silu(gate_ref[...])


def _conv(proj4, conv_w):
    _, bsz, seq, _ = proj4.shape
    cw = jnp.pad(conv_w, ((0, 5), (0, 0)))

    def spec(base):
        return pl.BlockSpec((None, None, seq, LANES), lambda b, c: (base + c, b, 0, 0))

    return pl.pallas_call(
        functools.partial(_conv_kernel, seq=seq),
        grid=(bsz, 4),
        in_specs=[pl.BlockSpec((8, LANES), lambda b, c: (0, c)),
                  spec(CB_CB), spec(CB_CC), spec(CB_CH), spec(CB_CGATE)],
        out_specs=pl.BlockSpec((None, None, seq, LANES), lambda b, c: (c, b, 0, 0)),
        out_shape=jax.ShapeDtypeStruct((4, bsz, seq, LANES), F32),
        scratch_shapes=[pltpu.VMEM((seq + 8, LANES), F32)],
        compiler_params=_cparams("arbitrary", "arbitrary"),
        name="conv",
    )(cw, proj4, proj4, proj4, proj4)


def _dil1_kernel(q_ref, k_ref, v_ref, bias_ref, o_ref, lse_ref, ka0, ka1, va0, va1, *, seq):
    lo = _lane_lo()
    ka, va = (ka0, ka1), (va0, va1)
    _store_kv_aug(k_ref[...], v_ref[...], ka, va, TILE, False)
    qfeat = _pad_feat()

    def qgroup(it, _):
        tiles = [it * DIL1_STREAMS + u for u in range(DIL1_STREAMS)]
        rows = [pl.ds(pl.multiple_of(i * TILE, TILE), TILE) for i in tiles]
        streams = [([(q_ref[r, :] * Q_SCALE).astype(BF16)], qfeat, i) for i, r in zip(tiles, rows)]
        res = _attend(streams, None, 2 * TILE, TILE, ka, va, lambda h, d: bias_ref[h, d])
        for (ms, accs), r in zip(res, rows):
            a0, l0, a1, l1 = _split_acc(accs, 0)
            o_ref[r, :] = jnp.where(lo, a0 / l0, a1 / l1)
            lse_ref[r, :] = jnp.where(lo, ms[0] + jnp.log2(l0), ms[1] + jnp.log2(l1))
        return 0

    lax.fori_loop(0, seq // (TILE * DIL1_STREAMS), qgroup, 0)


def _dil1(proj4, t_d1):
    _, bsz, seq, _ = proj4.shape

    def spec(base):
        return pl.BlockSpec((None, None, seq, LANES), lambda b, h: (base + h, b, 0, 0))

    ospec = pl.BlockSpec((None, None, seq, LANES), lambda b, h: (h, b, 0, 0))
    oshape = jax.ShapeDtypeStruct((4, bsz, seq, LANES), F32)
    return pl.pallas_call(
        functools.partial(_dil1_kernel, seq=seq),
        grid=(bsz, 4),
        in_specs=[spec(CB_DQ), spec(CB_DK), spec(CB_DV),
                  pl.BlockSpec((None, 2, 2, TILE, TILE), lambda b, h: (h, 0, 0, 0, 0))],
        out_specs=(ospec, ospec),
        out_shape=(oshape, oshape),
        scratch_shapes=[pltpu.VMEM((seq + TILE, LANES), BF16)] * 4,
        compiler_params=_cparams("arbitrary", "arbitrary"),
        name="dil1",
    )(proj4, proj4, proj4, t_d1)


def _dil_kernel(q_ref, k_ref, v_ref, o1_ref, lse1_ref, gate_ref, t4_ref, t16_ref, o_ref,
                qs, kp0, kp1, vp0, vp1, o1s, l1s, gs, res, *, seq):
    lo = _lane_lo()
    na = seq // PHASES
    kp, vp = (kp0, kp1), (vp0, vp1)

    def by_phase(ref):
        return pltpu.einshape("apl->pal", ref[...].reshape(na, PHASES, LANES))

    qs[...] = (by_phase(q_ref) * Q_SCALE).astype(BF16)
    k = by_phase(k_ref)
    kp0[...] = jnp.where(lo, k, 0.0).astype(BF16)
    kp1[...] = jnp.where(lo, 0.0, k).astype(BF16)
    v = by_phase(v_ref)
    vp0[...] = jnp.where(lo, v, 0.0).astype(BF16)
    vp1[...] = jnp.where(lo, 0.0, v).astype(BF16)
    o1s[...] = by_phase(o1_ref)
    l1s[...] = by_phase(lse1_ref)
    gs[...] = by_phase(gate_ref)

    def residue(r, _):
        phases = [4 * c + r for c in range(4)]
        q = jnp.concatenate([qs[p] for p in phases], axis=0)
        scores = [_dot_nt(q, jnp.concatenate([kp[h][p] for p in phases], axis=0)) for h in range(2)]
        e4, l4, m4, e16, l16, m16 = [], [], [], [], [], []
        for h in range(2):
            s = scores[h]
            for c in range(4):
                raw = [s[c * na:(c + 1) * na, c2 * na:(c2 + 1) * na] for c2 in range(4)]
                blks = [raw[c2] + t4_ref[h, c - c2 + 3] for c2 in range(4)]
                mx = jnp.maximum(jnp.maximum(blks[0], blks[1]), jnp.maximum(blks[2], blks[3]))
                m = jnp.max(mx, axis=-1, keepdims=True)
                eb = [jnp.exp2(b - m) for b in blks]
                l4.append(jnp.sum(eb[0] + eb[1] + eb[2] + eb[3], axis=-1, keepdims=True))
                m4.append(m)
                e4.append(jnp.concatenate([e.astype(BF16) for e in eb], axis=1))
                s16 = raw[c] + t16_ref[h]
                m = jnp.max(s16, axis=-1, keepdims=True)
                e = jnp.exp2(s16 - m)
                l16.append(jnp.sum(e, axis=-1, keepdims=True))
                m16.append(m)
                e16.append(e.astype(BF16))
        pv4 = [_dot(jnp.concatenate(e4[4 * h:4 * h + 4], axis=0),
                    jnp.concatenate([vp[h][p] for p in phases], axis=0)) for h in range(2)]
        pv16 = [[_dot(e16[4 * h + c], vp[h][phases[c]]) for c in range(4)] for h in range(2)]
        for c, p in enumerate(phases):
            o4 = pv4[0][c * na:(c + 1) * na, :] / l4[c] + pv4[1][c * na:(c + 1) * na, :] / l4[4 + c]
            o16 = pv16[0][c] / l16[c] + pv16[1][c] / l16[4 + c]
            lb = jnp.where(lo, m4[c] + jnp.log2(l4[c]), m4[4 + c] + jnp.log2(l4[4 + c]))
            lc = jnp.where(lo, m16[c] + jnp.log2(l16[c]), m16[4 + c] + jnp.log2(l16[4 + c]))
            la = l1s[p]
            mx = jnp.maximum(jnp.maximum(la, lb), lc)
            ea, eb, ec = jnp.exp2(la - mx), jnp.exp2(lb - mx), jnp.exp2(lc - mx)
            o = (ea * o1s[p] + eb * o4 + ec * o16) / (ea + eb + ec)
            res[p] = o * _silu(gs[p])
        return 0

    lax.fori_loop(0, 4, residue, 0)
    o_ref[...] = pltpu.einshape("pal->apl", res[...]).reshape(seq, LANES)


def _dil(proj4, o1, lse1, t_d4, t_d16):
    _, bsz, seq, _ = proj4.shape
    na = seq // PHASES

    def spec(base):
        return pl.BlockSpec((None, None, seq, LANES), lambda b, h: (base + h, b, 0, 0))

    return pl.pallas_call(
        functools.partial(_dil_kernel, seq=seq),
        grid=(bsz, 4),
        in_specs=[spec(CB_DQ), spec(CB_DK), spec(CB_DV), spec(0), spec(0), spec(CB_DGATE),
                  pl.BlockSpec((None, 2, 7, TILE, TILE), lambda b, h: (h, 0, 0, 0, 0)),
                  pl.BlockSpec((None, 2, TILE, TILE), lambda b, h: (h, 0, 0, 0))],
        out_specs=spec(0),
        out_shape=jax.ShapeDtypeStruct((4, bsz, seq, LANES), F32),
        scratch_shapes=[pltpu.VMEM((PHASES, na, LANES), BF16)] * 5 + [pltpu.VMEM((PHASES, na, LANES), F32)] * 4,
        compiler_params=_cparams("arbitrary", "arbitrary"),
        name="dil",
    )(proj4, proj4, proj4, o1, lse1, proj4, t_d4, t_d16)


def _compress_kernel(kv_ref, pos_ref, wlo_ref, whi_ref, w2_ref, o_ref, *, seq):
    na = seq // PHASES
    ph = pltpu.einshape("apl->pal", kv_ref[...].reshape(na, PHASES, LANES))
    r = jnp.concatenate([ph[p] for p in range(PHASES)], axis=1)
    h_lo = _dot((r + pos_ref[0:1, :]).astype(BF16), wlo_ref[...])
    h_hi = _dot((r + pos_ref[1:2, :]).astype(BF16), whi_ref[...])
    h = h_lo + pltpu.roll(h_hi, na - 1, axis=0)
    o_ref[...] = _dot(_silu(h).astype(BF16), w2_ref[...])


def _compress(proj4, pos2, w_lo, w_hi, w2):
    _, bsz, seq, _ = proj4.shape
    na = seq // PHASES
    wide = PHASES * LANES
    full = lambda shape: pl.BlockSpec(shape, lambda b, g: (0,) * len(shape))
    return pl.pallas_call(
        functools.partial(_compress_kernel, seq=seq),
        grid=(bsz, N_KV),
        in_specs=[pl.BlockSpec((None, None, seq, LANES), lambda b, g: (CB_ACMP + g, b, 0, 0)),
                  full((2, wide)), full((wide, 2 * CMP_HIDDEN)), full((wide, 2 * CMP_HIDDEN)),
                  full((2 * CMP_HIDDEN, LANES))],
        out_specs=pl.BlockSpec((None, None, na, LANES), lambda b, g: (b, g, 0, 0)),
        out_shape=jax.ShapeDtypeStruct((bsz, N_KV, na, LANES), F32),
        compiler_params=_cparams("arbitrary", "arbitrary"),
        name="compress",
    )(proj4, pos2, w_lo, w_hi, w2)


def _cmp_kernel(q_ref, kv_ref, bias_ref, ovt_ref, o_ref, feat_ref, *, seq):
    lo = _lane_lo()
    kv = kv_ref[...]
    k_lo = jnp.where(lo, kv, 0.0)
    v_hi = jnp.where(lo, 0.0, kv)
    kp = (k_lo.astype(BF16), pltpu.roll(k_lo, HEAD_DIM, axis=1).astype(BF16))
    vp = (pltpu.roll(v_hi, HEAD_DIM, axis=1).astype(BF16), v_hi.astype(BF16))
    ovt = ovt_ref[...]
    n_slc = ovt.shape[0]
    first = lax.broadcasted_iota(jnp.int32, (n_slc, CMP_ROWS), 0) == 0
    pad_rows = jnp.where(first, -1.0, 0.0).astype(BF16)
    eye = (lax.broadcasted_iota(jnp.int32, (TILE, TILE), 0)
           == lax.broadcasted_iota(jnp.int32, (TILE, TILE), 1)).astype(BF16)

    def qgroup(it, _):
        r0 = pl.multiple_of(it * CMP_ROWS, CMP_ROWS)
        rows = pl.ds(r0, CMP_ROWS)
        qs = [(q_ref[cb, rows, :] * Q_SCALE).astype(BF16) for cb in range(2)]
        scores = [[_dot_nt(qs[cb], kp[hh]) for hh in range(2)] for cb in range(2)]
        probs = [[None, None], [None, None]]
        psum = jnp.zeros((CMP_ROWS, LANES), F32)
        for cb in range(2):
            for hh in range(2):
                s = scores[cb][hh] + bias_ref[2 * cb + hh, rows, :]
                m = jnp.maximum(jnp.max(s, axis=-1, keepdims=True), -1e20)
                e = jnp.exp2(s - m)
                p = e / jnp.maximum(jnp.sum(e, axis=-1, keepdims=True), 1e-30)
                psum = psum + p
                probs[cb][hh] = p.astype(BF16)
        for cb in range(2):
            o_ref[cb, rows, :] = _dot(probs[cb][0], vp[0]) + _dot(probs[cb][1], vp[1])
        p1 = psum.astype(BF16)
        rem = psum - p1.astype(F32)
        p2 = rem.astype(BF16)
        p3 = (rem - p2.astype(F32)).astype(BF16)
        imp = _dot_nt(ovt, p1) + _dot_nt(ovt, p2) + _dot_nt(ovt, p3)
        blk = lax.broadcasted_iota(jnp.int32, (n_slc, CMP_ROWS), 0)
        t = r0 + lax.broadcasted_iota(jnp.int32, (n_slc, CMP_ROWS), 1)
        cur = t // SLC_BLK
        forced = (blk == 0) | (blk == cur) | (blk == cur - 1)
        imp = jnp.where(blk > cur, NEG, jnp.where(forced, FORCE, imp))
        rank = jnp.zeros((n_slc, CMP_ROWS), F32)
        for mth in range(n_slc):
            row = imp[mth:mth + 1, :]
            tie = (blk > mth).astype(F32)
            rank = rank + jnp.where(row > imp, 1.0, jnp.where(row == imp, tie, 0.0))
        unsel = jnp.where(rank < N_SEL, 0.0, -1.0).astype(BF16)
        half = jnp.concatenate([unsel, pad_rows], axis=0)
        feat_t = jnp.concatenate([half, half], axis=0)
        for u in range(CMP_ROWS // TILE):
            piece = _dot_nt(eye, feat_t[:, u * TILE:(u + 1) * TILE])
            feat_ref[pl.ds(r0 + u * TILE, TILE), :] = piece.astype(BF16)
        return 0

    lax.fori_loop(0, seq // CMP_ROWS, qgroup, 0)


def _cmp_select(proj4, kvcmp, t_cmp, ovt):
    _, bsz, seq, _ = proj4.shape
    return pl.pallas_call(
        functools.partial(_cmp_kernel, seq=seq),
        grid=(bsz, N_KV),
        in_specs=[pl.BlockSpec((2, None, seq, LANES), lambda b, g: (CB_AQ // 2 + g, b, 0, 0)),
                  pl.BlockSpec((None, None, TILE, LANES), lambda b, g: (b, g, 0, 0)),
                  pl.BlockSpec((4, seq, LANES), lambda b, g: (g, 0, 0)),
                  pl.BlockSpec(ovt.shape, lambda b, g: (0, 0))],
        out_specs=(pl.BlockSpec((2, None, seq, LANES), lambda b, g: (g, b, 0, 0)),
                   pl.BlockSpec((None, None, seq, LANES), lambda b, g: (b, g, 0, 0))),
        out_shape=(jax.ShapeDtypeStruct((4, bsz, seq, LANES), F32),
                   jax.ShapeDtypeStruct((bsz, N_KV, seq, LANES), BF16)),
        compiler_params=_cparams("arbitrary", "arbitrary"),
        name="cmp_select",
    )(proj4, kvcmp, t_cmp, ovt)


def _slcwin_kernel(q_ref, ks_ref, vs_ref, kw_ref, vw_ref, ocmp_ref, feat_ref, gates_ref, agate_ref,
                   tslc_ref, twin_ref, x_ref, o_ref,
                   ksa0, ksa1, vsa0, vsa1, kwa0, kwa1, vwa0, vwa1, *, seq):
    g = pl.program_id(1)
    lo = _lane_lo()
    ksa, vsa, kwa, vwa = (ksa0, ksa1), (vsa0, vsa1), (kwa0, kwa1), (vwa0, vwa1)
    slc_pad = SLC_CHUNK - TILE
    win_pad = (WIN_TILES - 1) * TILE
    _store_kv_aug(_group_both_halves(g, ks_ref[...]), _group_both_halves(g, vs_ref[...]),
                  ksa, vsa, slc_pad, True)
    _store_kv_aug(_group_both_halves(g, kw_ref[...]), _group_both_halves(g, vw_ref[...]),
                  kwa, vwa, win_pad, False)
    slc_tiles = SLC_CHUNK // TILE

    def qgroup(it, _):
        tiles = [it * SLC_STREAMS + u for u in range(SLC_STREAMS)]
        rows = [pl.ds(pl.multiple_of(i * TILE, TILE), TILE) for i in tiles]
        streams = [([(q_ref[cb, r, :] * Q_SCALE).astype(BF16) for cb in range(2)],
                    feat_ref[r, :], i) for i, r in zip(tiles, rows)]
        n_chunks = (tiles[0] + slc_tiles) // slc_tiles
        res_w = _attend(streams, None, WIN_TILES * TILE, win_pad, kwa, vwa, lambda h, d: twin_ref[h, d])
        res_s = _attend(streams, n_chunks, SLC_CHUNK, slc_pad, ksa, vsa, lambda h, d: tslc_ref[h, d])
        for (_, acc_s), (_, acc_w), r in zip(res_s, res_w, rows):
            sg = jax.nn.sigmoid(gates_ref[r, :])
            sg1 = sg.astype(BF16)
            sg2 = (sg - sg1.astype(F32)).astype(BF16)
            for cb in range(2):
                a0, l0, a1, l1 = _split_acc(acc_s, cb)
                o_slc = jnp.where(lo, a0 / l0, a1 / l1)
                a0, l0, a1, l1 = _split_acc(acc_w, cb)
                o_win = jnp.where(lo, a0 / l0, a1 / l1)
                gts = [_dot(sg1, x_ref[br, cb]) + _dot(sg2, x_ref[br, cb]) for br in range(3)]
                o = gts[0] * ocmp_ref[cb, r, :] + gts[1] * o_slc + gts[2] * o_win
                o_ref[cb, r, :] = o * _silu(agate_ref[cb, r, :])
        return 0

    assert slc_tiles % SLC_STREAMS == 0
    lax.fori_loop(0, seq // (TILE * SLC_STREAMS), qgroup, 0)


def _slcwin(proj4, o_cmp, feat, t_slc, t_win, x_tab):
    _, bsz, seq, _ = proj4.shape
    nt = seq // TILE

    def one(cb):
        return pl.BlockSpec((None, None, seq, LANES), lambda b, g: (cb, b, 0, 0))

    def two(base):
        return pl.BlockSpec((2, None, seq, LANES), lambda b, g: (base // 2 + g, b, 0, 0))

    slc_rows = seq + SLC_CHUNK - TILE
    win_rows = seq + (WIN_TILES - 1) * TILE
    return pl.pallas_call(
        functools.partial(_slcwin_kernel, seq=seq),
        grid=(bsz, N_KV),
        in_specs=[two(CB_AQ), one(CB_AKS), one(CB_AVS), one(CB_AKW), one(CB_AVW),
                  pl.BlockSpec((2, None, seq, LANES), lambda b, g: (g, b, 0, 0)),
                  pl.BlockSpec((None, None, seq, LANES), lambda b, g: (b, g, 0, 0)),
                  one(CB_AGATES), two(CB_AGATE),
                  pl.BlockSpec((4, nt, TILE, TILE), lambda b, g: (g, 0, 0, 0)),
                  pl.BlockSpec((4, WIN_TILES, TILE, TILE), lambda b, g: (g, 0, 0, 0)),
                  pl.BlockSpec((None, 3, 2, TILE, TILE), lambda b, g: (g, 0, 0, 0, 0))],
        out_specs=pl.BlockSpec((2, None, seq, LANES), lambda b, g: (g, b, 0, 0)),
        out_shape=jax.ShapeDtypeStruct((4, bsz, seq, LANES), F32),
        scratch_shapes=[pltpu.VMEM((slc_rows, LANES), BF16)] * 4 + [pltpu.VMEM((win_rows, LANES), BF16)] * 4,
        compiler_params=_cparams("arbitrary", "arbitrary"),
        name="slcwin",
    )(proj4, proj4, proj4, proj4, proj4, o_cmp, feat, proj4, proj4, t_slc, t_win, x_tab)


def _toeplitz_kernel(par_ref, v_ref, o_ref, *, n_tiles, interleave):
    a = lax.broadcasted_iota(jnp.int32, (TILE, TILE), 0)
    b = lax.broadcasted_iota(jnp.int32, (TILE, TILE), 1)
    for t in range(n_tiles):
        x = jnp.broadcast_to(v_ref[t:t + 1, :], (TILE, 2 * TILE))
        y = pltpu.roll(x, 0, 1, stride=1, stride_axis=0)[:, :TILE]
        dist = par_ref[t, 0] + par_ref[t, 1] * (a - b)
        ok = (dist >= par_ref[t, 2]) & (dist <= par_ref[t, 3]) & (b < par_ref[t, 4])
        tile = jnp.where(ok, y * LOG2E, NEG)
        if interleave:
            o_ref[pl.ds(t, TILE, stride=n_tiles), :] = tile
        else:
            o_ref[t] = tile


def _toeplitz(v, params, interleave=False):
    heads, n_tiles, _ = v.shape
    if interleave:
        oshape, ospec = (heads, n_tiles * TILE, TILE), pl.BlockSpec((None, n_tiles * TILE, TILE), lambda h: (h, 0, 0))
    else:
        oshape, ospec = (heads, n_tiles, TILE, TILE), pl.BlockSpec((None, n_tiles, TILE, TILE), lambda h: (h, 0, 0, 0))
    return pl.pallas_call(
        functools.partial(_toeplitz_kernel, n_tiles=n_tiles, interleave=interleave),
        grid=(heads,),
        in_specs=[pl.BlockSpec(memory_space=pltpu.SMEM),
                  pl.BlockSpec((None, n_tiles, 2 * TILE), lambda h: (h, 0, 0))],
        out_specs=ospec,
        out_shape=jax.ShapeDtypeStruct(oshape, F32),
        compiler_params=_cparams("arbitrary"),
        name="toeplitz",
    )(jnp.asarray(params, jnp.int32), v)


def _t5_bucket(d):
    exact = NUM_BUCKETS // 2
    large = exact + (jnp.log(jnp.maximum(d, exact).astype(F32) / exact)
                     / math.log(MAX_DISTANCE / exact) * (NUM_BUCKETS - exact)).astype(jnp.int32)
    return jnp.where(d < exact, d, jnp.minimum(large, NUM_BUCKETS - 1))


def _generator_rows(ext, off, base, mul):
    lo = off + base - 127 * mul
    first = ext[:, lo:lo + 127 * mul + 1:mul][:, ::-1]
    second = ext[:, off + base + mul:off + base + 128 * mul + 1:mul][:, ::-1]
    return jnp.concatenate([first, second], axis=1)


def _bias_tables(rel_bias, seq):
    big = 1 << 30
    onehot = jax.nn.one_hot(_t5_bucket(jnp.arange(seq, dtype=jnp.int32)), NUM_BUCKETS, dtype=F32)
    bias_d = jnp.dot(onehot, rel_bias, precision=lax.Precision.HIGHEST).T
    off = 128 * PHASES + 64
    ext = jnp.pad(bias_d, ((0, 0), (off, off)))
    ba, bb, bd = ext[:8], ext[8:16], ext[16:24]
    nt = seq // TILE

    def plain(e, n, hi):
        v = jnp.stack([_generator_rows(e, off, TILE * d, 1) for d in range(n)], axis=1)
        return _toeplitz(v, [(TILE * d, 1, 0, hi, TILE) for d in range(n)])

    t_slc = plain(ba, nt, big)
    t_win = plain(ba, WIN_TILES, WIN_A - 1)
    t_b = plain(bb, 2, WIN_B - 1)
    t_d1 = plain(bd, 2, DIL_MAXDIST)
    v4 = jnp.stack([_generator_rows(bd, off, 4 * dl, 16) for dl in range(-3, 4)], axis=1)
    t_d4 = _toeplitz(v4, [(dl, 4, 0, DIL_MAXDIST, TILE) for dl in range(-3, 4)])
    v16 = jnp.stack([_generator_rows(bd, off, 0, 16)], axis=1)
    t_d16 = _toeplitz(v16, [(0, 1, 0, big, TILE)])
    n_cmp = (seq - CMP_BLK) // CMP_STRIDE + 1
    vc = jnp.stack([_generator_rows(ba, off, p - (CMP_BLK - 1), CMP_STRIDE) for p in range(PHASES)], axis=1)
    t_cmp = _toeplitz(vc, [(p - (CMP_BLK - 1), CMP_STRIDE, 0, big, n_cmp) for p in range(PHASES)],
                      interleave=True)
    return dict(slc=t_slc, win=t_win, b=t_b, d1=t_d1.reshape(4, 2, 2, TILE, TILE),
                d4=t_d4.reshape(4, 2, 7, TILE, TILE), d16=t_d16.reshape(4, 2, TILE, TILE), cmp=t_cmp)


def _const_tables(seq):
    n_cmp = (seq - CMP_BLK) // CMP_STRIDE + 1
    n_slc = seq // SLC_BLK
    c0 = np.arange(TILE)[None, :] * CMP_STRIDE
    s0 = np.arange(n_slc)[:, None] * SLC_BLK
    ovt = ((c0 < s0 + SLC_BLK) & (c0 + CMP_BLK > s0) & (np.arange(TILE)[None, :] < n_cmp))
    row = np.arange(TILE)[None, None, None, :, None]
    lane = np.arange(TILE)[None, None, None, None, :]
    g = np.arange(N_KV)[:, None, None, None, None]
    br = np.arange(3)[None, :, None, None, None]
    cb = np.arange(2)[None, None, :, None, None]
    x_tab = (row == br * N_HEADS + 4 * g + 2 * cb + lane // HEAD_DIM)
    as_bf = lambda m: jnp.asarray(m.astype(np.float32), BF16)
    return as_bf(ovt), as_bf(x_tab)


_W_IN_SRC = ([0, 128, 256, 384, None, None, 768, 896, 1024, 1152]
             + [1304 + LANES * k for k in range(N_CB - 11)] + [None])


def _prep_w_in_kernel(w_ref, o_ref):
    rows = w_ref.shape[0]
    for b, c0 in enumerate(_W_IN_SRC):
        if c0 is not None:
            blk = w_ref[:, c0:c0 + LANES]
        elif b == CB_ACMP:
            blk = jnp.concatenate([w_ref[:, 512:576], w_ref[:, 640:704]], axis=1)
        elif b == CB_ACMP + 1:
            blk = jnp.concatenate([w_ref[:, 576:640], w_ref[:, 704:768]], axis=1)
        else:
            blk = jnp.concatenate([w_ref[:, 1280:1304], jnp.zeros((rows, LANES - 24), F32)], axis=1)
        o_ref[:, b * LANES:(b + 1) * LANES] = blk.astype(BF16)


def _prep_w_in(w_in, tr=256):
    depth, d, _ = w_in.shape
    return pl.pallas_call(
        _prep_w_in_kernel,
        grid=(depth, d // tr),
        in_specs=[pl.BlockSpec((None, tr, D_IN), lambda l, i: (l, i, 0))],
        out_specs=pl.BlockSpec((None, tr, N_CB * LANES), lambda l, i: (l, i, 0)),
        out_shape=jax.ShapeDtypeStruct((depth, d, N_CB * LANES), BF16),
        compiler_params=_cparams("arbitrary", "arbitrary"),
        name="prep_w_in",
    )(w_in)


def _compress_weights(cmp_pos, cmp_w1, cmp_w2):
    half = CMP_BLK // 2
    w1 = cmp_w1.reshape(2, CMP_BLK, HEAD_DIM, CMP_HIDDEN)
    z = jnp.zeros((half, HEAD_DIM, CMP_HIDDEN), F32)

    def stack(part):
        top = jnp.concatenate([w1[0, part], z], axis=-1)
        bot = jnp.concatenate([z, w1[1, part]], axis=-1)
        return jnp.concatenate([top, bot], axis=1).reshape(half * LANES, 2 * CMP_HIDDEN).astype(BF16)

    w_lo, w_hi = stack(slice(0, half)), stack(slice(half, CMP_BLK))
    z2 = jnp.zeros((CMP_HIDDEN, HEAD_DIM), F32)
    w2 = jnp.concatenate([jnp.concatenate([cmp_w2[0], z2], axis=1),
                          jnp.concatenate([z2, cmp_w2[1]], axis=1)], axis=0).astype(BF16)
    pos = jnp.concatenate([cmp_pos[0], cmp_pos[1]], axis=-1)
    pos2 = pos.reshape(2, half * LANES)
    return pos2, w_lo, w_hi, w2


def _layer(x2, bsz, seq, layer, norm_w, w_perm, w_out, conv_w, sinks, cmp_pos, cmp_w1, cmp_w2, final_w, final,
           tabs, consts):
    ovt, x_tab = consts
    proj = _inproj(x2, norm_w, w_perm, layer)
    proj4 = proj.reshape(N_CB, bsz, seq, LANES)
    kvcmp = _compress(proj4, *_compress_weights(cmp_pos, cmp_w1, cmp_w2))
    o_cmp, feat = _cmp_select(proj4, kvcmp, tabs["cmp"], ovt)
    mix_a = _slcwin(proj4, o_cmp, feat, tabs["slc"], tabs["win"], x_tab)
    mix_b = _swa(proj4, sinks, tabs["b"])
    mix_c = _conv(proj4, conv_w)
    o1, lse1 = _dil1(proj4, tabs["d1"])
    mix_d = _dil(proj4, o1, lse1, tabs["d4"], tabs["d16"])
    flat = lambda t: t.reshape(4, bsz * seq, LANES)
    return _outproj(flat(mix_a), flat(mix_b), flat(mix_c), flat(mix_d), w_out.astype(BF16), x2,
                    final_w, final)


def kernel(x, norm_w, w_in, w_out, conv_w, sinks, cmp_pos, cmp_w1, cmp_w2, rel_bias, final_norm_w):
    bsz, seq, _ = x.shape
    depth = norm_w.shape[0]
    tabs = _bias_tables(rel_bias, seq)
    consts = _const_tables(seq)
    w_perm = _prep_w_in(w_in)
    x2 = x.reshape(bsz * seq, D_MODEL)
    for layer in range(depth):
        x2 = _layer(x2, bsz, seq, layer, norm_w[layer], w_perm, w_out[layer], conv_w[layer], sinks[layer],
                    cmp_pos[layer], cmp_w1[layer], cmp_w2[layer], final_norm_w, layer == depth - 1,
                    tabs, consts)
    return x2.reshape(bsz, seq, D_MODEL)
```

```python
import functools
import math

import numpy as np
import jax
import jax.numpy as jnp
from jax import lax
from jax.experimental import pallas as pl
from jax.experimental.pallas import tpu as pltpu

F32 = jnp.float32
BF16 = jnp.bfloat16

D_MODEL = 2048
HEAD_DIM = 64
N_HEADS = 8
N_KV = 2
CMP_BLK = 32
CMP_STRIDE = 16
CMP_HIDDEN = 128
SLC_BLK = 64
N_SEL = 8
WIN_A = 512
WIN_B = 128
DIL_MAXDIST = 128
NUM_BUCKETS = 32
MAX_DISTANCE = 2048
RMS_EPS = 1e-6
NEG = -1e30
BIG = 1e30
FORCE = 1e4
D_IN = 7192
PHASES = 16
LOG2E = math.log2(math.e)
Q_SCALE = HEAD_DIM ** -0.5 * LOG2E

LANES = 128
TILE = 128
VMEM_LIMIT = 56 * 1024 * 1024

SEL_LANE = (HEAD_DIM, 0)
PAD_LANE = (HEAD_DIM + 32, 32)
ONE_LANE = (HEAD_DIM, 0)

CB_AQ = 0
CB_ACMP = 4
CB_AKS, CB_AVS, CB_AKW, CB_AVW = 6, 7, 8, 9
CB_AGATE = 10
CB_BQ, CB_BK, CB_BV, CB_BGATE = 14, 18, 19, 20
CB_CB, CB_CC, CB_CH, CB_CGATE = 24, 28, 32, 36
CB_DQ, CB_DK, CB_DV, CB_DGATE = 40, 44, 48, 52
CB_AGATES = 56
N_CB = 57
CB_PER_STEP = 19

SLC_CHUNK = 4 * TILE
WIN_TILES = 5
SWA_STREAMS = 4
DIL1_STREAMS = 8
SLC_STREAMS = 4
CMP_ROWS = 8 * TILE
CONV_ROWS = 2 * TILE


def _cparams(*sem):
    return pltpu.CompilerParams(dimension_semantics=sem, vmem_limit_bytes=VMEM_LIMIT)


def _lane():
    return lax.broadcasted_iota(jnp.int32, (1, LANES), 1)


def _lane_lo():
    return _lane() < HEAD_DIM


def _dot_nt(a, b):
    return lax.dot_general(a, b, (((1,), (1,)), ((), ())), preferred_element_type=F32)


def _dot(a, b):
    return jnp.dot(a, b, preferred_element_type=F32)


def _silu(x):
    return x * jax.nn.sigmoid(x)


def _inproj_kernel(x_ref, nw_ref, w_ref, o_ref):
    x = x_ref[...]
    y = x * lax.rsqrt(jnp.mean(x * x, axis=-1, keepdims=True) + RMS_EPS)
    xn = (y * nw_ref[...]).astype(BF16)
    acc = _dot(xn, w_ref[...])
    for k in range(CB_PER_STEP):
        o_ref[k] = acc[:, k * LANES:(k + 1) * LANES]


def _inproj(x2, norm_w, w_perm, layer, tm=512):
    m = x2.shape[0]
    n_steps = N_CB // CB_PER_STEP
    return pl.pallas_call(
        _inproj_kernel,
        grid=(n_steps, m // tm),
        in_specs=[
            pl.BlockSpec((tm, D_MODEL), lambda j, i: (i, 0)),
            pl.BlockSpec((1, D_MODEL), lambda j, i: (0, 0)),
            pl.BlockSpec((None, D_MODEL, CB_PER_STEP * LANES), lambda j, i: (layer, 0, j)),
        ],
        out_specs=pl.BlockSpec((CB_PER_STEP, tm, LANES), lambda j, i: (j, i, 0)),
        out_shape=jax.ShapeDtypeStruct((N_CB, m, LANES), F32),
        compiler_params=_cparams("arbitrary", "arbitrary"),
        name="inproj",
    )(x2, norm_w.reshape(1, D_MODEL), w_perm)


def _outproj_kernel(a_ref, b_ref, c_ref, d_ref, w_ref, x_ref, fw_ref, o_ref, *, final):
    mix = jnp.concatenate([r[k] for r in (a_ref, b_ref, c_ref, d_ref) for k in range(4)], axis=1)
    y = x_ref[...] + _dot(mix.astype(BF16), w_ref[...])
    if final:
        y = y * lax.rsqrt(jnp.mean(y * y, axis=-1, keepdims=True) + RMS_EPS) * fw_ref[...]
    o_ref[...] = y


def _outproj(mix_a, mix_b, mix_c, mix_d, w_out_bf, x2, final_w, final, tm=512):
    m = x2.shape[0]
    mspec = pl.BlockSpec((4, tm, LANES), lambda i: (0, i, 0))
    return pl.pallas_call(
        functools.partial(_outproj_kernel, final=final),
        grid=(m // tm,),
        in_specs=[mspec, mspec, mspec, mspec,
                  pl.BlockSpec((D_MODEL, D_MODEL), lambda i: (0, 0)),
                  pl.BlockSpec((tm, D_MODEL), lambda i: (i, 0)),
                  pl.BlockSpec((1, D_MODEL), lambda i: (0, 0))],
        out_specs=pl.BlockSpec((tm, D_MODEL), lambda i: (i, 0)),
        out_shape=jax.ShapeDtypeStruct((m, D_MODEL), F32),
        compiler_params=_cparams("arbitrary"),
        name="outproj",
    )(mix_a, mix_b, mix_c, mix_d, w_out_bf, x2, final_w.reshape(1, D_MODEL))


def _pad_feat():
    lane = _lane()
    row = jnp.where((lane == PAD_LANE[0]) | (lane == PAD_LANE[1]), -1.0, 0.0)
    return jnp.broadcast_to(row, (TILE, LANES)).astype(BF16)


def _store_kv_aug(k, v, ka, va, pad_rows, with_blocks):
    lo = _lane_lo()
    lane = _lane()
    seq = k.shape[0]
    blk = lax.broadcasted_iota(jnp.int32, (seq, 1), 0) // SLC_BLK
    for h in range(2):
        own = lo if h == 0 else jnp.logical_not(lo)
        kfeat = jnp.where(lane - SEL_LANE[h] == blk, BIG, 0.0) if with_blocks else 0.0
        vfeat = jnp.where(lane == ONE_LANE[h], 1.0, 0.0)
        ka[h][pad_rows:pad_rows + seq, :] = jnp.where(own, k, kfeat).astype(BF16)
        va[h][pad_rows:pad_rows + seq, :] = jnp.where(own, v, vfeat).astype(BF16)
        flag = jnp.where(lane == PAD_LANE[h], BIG, 0.0)
        ka[h][0:pad_rows, :] = jnp.broadcast_to(flag, (pad_rows, LANES)).astype(BF16)
        va[h][0:pad_rows, :] = jnp.zeros((pad_rows, LANES), BF16)


def _group_both_halves(g, x):
    own = jnp.where(_lane_lo() == (g == 0), x, 0.0)
    return own + pltpu.roll(own, HEAD_DIM, axis=1)


def _attend(streams, n_chunks, kc, pad_rows, ka, va, bias_fn):
    lo = _lane_lo()
    ncb = len(streams[0][0])
    nt = kc // TILE
    qas = [(jnp.concatenate([jnp.where(lo, q, qfeat) for q in q_list], axis=0),
            jnp.concatenate([jnp.where(lo, qfeat, q) for q in q_list], axis=0))
           for q_list, qfeat, _ in streams]

    def chunk_starts(k):
        return [pl.multiple_of(pad_rows + (i + 1) * TILE - (k + 1) * kc, TILE) for _, _, i in streams]

    def chunk_scores(k):
        return tuple(tuple(_dot_nt(qa[h], ka[h][pl.ds(start, kc), :]) for h in range(2))
                     for qa, start in zip(qas, chunk_starts(k)))

    def step(k, scores, carry):
        starts = chunk_starts(k)
        bias = {}
        for h in range(2):
            for cb in range(ncb):
                for jt in range(nt):
                    bias[h, cb, jt] = bias_fn(2 * cb + h, k * nt + (nt - 1 - jt))
        probs, new_ms, alphas = [], [], []
        for si, (ms, _) in enumerate(carry):
            ms_s = [None] * (2 * ncb)
            p_s, a_s = [], []
            for h in range(2):
                s = scores[si][h]
                ps, al = [], []
                for cb in range(ncb):
                    blks = [s[cb * TILE:(cb + 1) * TILE, jt * TILE:(jt + 1) * TILE] + bias[h, cb, jt]
                            for jt in range(nt)]
                    mx = blks[0]
                    for b in blks[1:]:
                        mx = jnp.maximum(mx, b)
                    m_old = ms[2 * cb + h]
                    m_new = jnp.maximum(m_old, jnp.max(mx, axis=-1, keepdims=True))
                    al.append(jnp.broadcast_to(jnp.exp2(m_old - m_new), (TILE, LANES)))
                    ps.append(jnp.concatenate([jnp.exp2(b - m_new).astype(BF16) for b in blks], axis=1))
                    ms_s[2 * cb + h] = m_new
                p_s.append(jnp.concatenate(ps, axis=0))
                a_s.append(jnp.concatenate(al, axis=0))
            probs.append(p_s)
            alphas.append(a_s)
            new_ms.append(tuple(ms_s))
        out = []
        for si, (_, accs) in enumerate(carry):
            new_accs = tuple(accs[h] * alphas[si][h] + _dot(probs[si][h], va[h][pl.ds(starts[si], kc), :])
                             for h in range(2))
            out.append((new_ms[si], new_accs))
        return tuple(out)

    init = tuple((tuple(jnp.full((TILE, 1), NEG, F32) for _ in range(2 * ncb)),
                  tuple(jnp.zeros((ncb * TILE, LANES), F32) for _ in range(2))) for _ in streams)
    if n_chunks is None:
        return step(0, chunk_scores(0), init)

    return lax.fori_loop(0, n_chunks, lambda k, carry: step(k, chunk_scores(k), carry), init)


def _split_acc(accs, cb):
    a0 = accs[0][cb * TILE:(cb + 1) * TILE, :]
    a1 = accs[1][cb * TILE:(cb + 1) * TILE, :]
    return a0, a0[:, ONE_LANE[0]:ONE_LANE[0] + 1], a1, a1[:, ONE_LANE[1]:ONE_LANE[1] + 1]


def _swa_conv_kernel(sink_ref, q_ref, k_ref, v_ref, gate_ref, bias_ref, cw_ref, cbb_ref, cc_ref, ch_ref, cg_ref,
                     o_ref, oc_ref, ka0, ka1, va0, va1, pad_ref, *, seq):
    g = pl.program_id(1)
    lo = _lane_lo()
    pad_ref[0:8, :] = jnp.zeros((8, LANES), F32)
    for cb in range(2):
        w = cw_ref[:, cb * LANES:(cb + 1) * LANES]
        for r in range(0, seq, CONV_ROWS):
            pad_ref[8 + r:8 + r + CONV_ROWS, :] = (cc_ref[cb, r:r + CONV_ROWS, :]
                                                   * ch_ref[cb, r:r + CONV_ROWS, :])
        for r in range(0, seq, CONV_ROWS):
            y = (w[0:1, :] * pad_ref[6 + r:6 + r + CONV_ROWS, :]
                 + w[1:2, :] * pad_ref[7 + r:7 + r + CONV_ROWS, :]
                 + w[2:3, :] * pad_ref[8 + r:8 + r + CONV_ROWS, :])
            oc_ref[cb, r:r + CONV_ROWS, :] = (cbb_ref[cb, r:r + CONV_ROWS, :] * y
                                              * _silu(cg_ref[cb, r:r + CONV_ROWS, :]))

    ka, va = (ka0, ka1), (va0, va1)
    _store_kv_aug(_group_both_halves(g, k_ref[...]), _group_both_halves(g, v_ref[...]), ka, va, TILE, False)
    qfeat = _pad_feat()

    def qgroup(it, _):
        tiles = [it * SWA_STREAMS + u for u in range(SWA_STREAMS)]
        rows = [pl.ds(pl.multiple_of(i * TILE, TILE), TILE) for i in tiles]
        streams = [([(q_ref[cb, r, :] * Q_SCALE).astype(BF16) for cb in range(2)], qfeat, i)
                   for i, r in zip(tiles, rows)]
        res = _attend(streams, None, 2 * TILE, TILE, ka, va, lambda h, d: bias_ref[h, d])
        for (ms, accs), r in zip(res, rows):
            for cb in range(2):
                a0, l0, a1, l1 = _split_acc(accs, cb)
                outs = []
                for h, (a, l) in enumerate(((a0, l0), (a1, l1))):
                    m = ms[2 * cb + h]
                    sink = sink_ref[g * 4 + 2 * cb + h] * LOG2E
                    mt = jnp.maximum(m, sink)
                    scale = jnp.exp2(m - mt)
                    outs.append(a * (scale / (l * scale + jnp.exp2(sink - mt))))
                o = jnp.where(lo, outs[0], outs[1])
                o_ref[cb, r, :] = o * _silu(gate_ref[cb, r, :])
        return 0

    lax.fori_loop(0, seq // (TILE * SWA_STREAMS), qgroup, 0)


def _swa_conv(proj4, sinks, t_b, conv_w):
    _, bsz, seq, _ = proj4.shape
    cw = jnp.pad(conv_w, ((0, 5), (0, 0)))

    def two(base):
        return pl.BlockSpec((2, None, seq, LANES), lambda b, g: (base // 2 + g, b, 0, 0))

    def one(cb):
        return pl.BlockSpec((None, None, seq, LANES), lambda b, g: (cb, b, 0, 0))

    ospec = pl.BlockSpec((2, None, seq, LANES), lambda b, g: (g, b, 0, 0))
    oshape = jax.ShapeDtypeStruct((4, bsz, seq, LANES), F32)
    return pl.pallas_call(
        functools.partial(_swa_conv_kernel, seq=seq),
        grid=(bsz, N_KV),
        in_specs=[
            pl.BlockSpec(memory_space=pltpu.SMEM),
            two(CB_BQ), one(CB_BK), one(CB_BV), two(CB_BGATE),
            pl.BlockSpec((4, 2, TILE, TILE), lambda b, g: (g, 0, 0, 0)),
            pl.BlockSpec((8, 2 * LANES), lambda b, g: (0, g)),
            two(CB_CB), two(CB_CC), two(CB_CH), two(CB_CGATE),
        ],
        out_specs=(ospec, ospec),
        out_shape=(oshape, oshape),
        scratch_shapes=[pltpu.VMEM((seq + TILE, LANES), BF16)] * 4 + [pltpu.VMEM((seq + 8, LANES), F32)],
        compiler_params=_cparams("arbitrary", "arbitrary"),
        name="swa_conv",
    )(sinks, proj4, proj4, proj4, proj4, t_b, cw, proj4, proj4, proj4, proj4)


def _dil1_kernel(q_ref, k_ref, v_ref, bias_ref, o_ref, lse_ref, ka0, ka1, va0, va1, *, seq):
    lo = _lane_lo()
    ka, va = (ka0, ka1), (va0, va1)
    _store_kv_aug(k_ref[...], v_ref[...], ka, va, TILE, False)
    qfeat = _pad_feat()

    def qgroup(it, _):
        tiles = [it * DIL1_STREAMS + u for u in range(DIL1_STREAMS)]
        rows = [pl.ds(pl.multiple_of(i * TILE, TILE), TILE) for i in tiles]
        streams = [([(q_ref[r, :] * Q_SCALE).astype(BF16)], qfeat, i) for i, r in zip(tiles, rows)]
        res = _attend(streams, None, 2 * TILE, TILE, ka, va, lambda h, d: bias_ref[h, d])
        for (ms, accs), r in zip(res, rows):
            a0, l0, a1, l1 = _split_acc(accs, 0)
            o_ref[r, :] = jnp.where(lo, a0 / l0, a1 / l1)
            lse_ref[r, :] = jnp.where(lo, ms[0] + jnp.log2(l0), ms[1] + jnp.log2(l1))
        return 0

    lax.fori_loop(0, seq // (TILE * DIL1_STREAMS), qgroup, 0)


def _dil1(proj4, t_d1):
    _, bsz, seq, _ = proj4.shape

    def spec(base):
        return pl.BlockSpec((None, None, seq, LANES), lambda b, h: (base + h, b, 0, 0))

    ospec = pl.BlockSpec((None, None, seq, LANES), lambda b, h: (h, b, 0, 0))
    oshape = jax.ShapeDtypeStruct((4, bsz, seq, LANES), F32)
    return pl.pallas_call(
        functools.partial(_dil1_kernel, seq=seq),
        grid=(bsz, 4),
        in_specs=[spec(CB_DQ), spec(CB_DK), spec(CB_DV),
                  pl.BlockSpec((None, 2, 2, TILE, TILE), lambda b, h: (h, 0, 0, 0, 0))],
        out_specs=(ospec, ospec),
        out_shape=(oshape, oshape),
        scratch_shapes=[pltpu.VMEM((seq + TILE, LANES), BF16)] * 4,
        compiler_params=_cparams("arbitrary", "arbitrary"),
        name="dil1",
    )(proj4, proj4, proj4, t_d1)


def _dil_kernel(q_ref, k_ref, v_ref, o1_ref, lse1_ref, gate_ref, t4_ref, t16_ref, o_ref,
                qs, kp0, kp1, vp0, vp1, o1s, l1s, gs, res, *, seq):
    lo = _lane_lo()
    na = seq // PHASES
    kp, vp = (kp0, kp1), (vp0, vp1)

    def by_phase(ref):
        return pltpu.einshape("apl->pal", ref[...].reshape(na, PHASES, LANES))

    qs[...] = (by_phase(q_ref) * Q_SCALE).astype(BF16)
    k = by_phase(k_ref)
    kp0[...] = jnp.where(lo, k, 0.0).astype(BF16)
    kp1[...] = jnp.where(lo, 0.0, k).astype(BF16)
    v = by_phase(v_ref)
    vp0[...] = jnp.where(lo, v, 0.0).astype(BF16)
    vp1[...] = jnp.where(lo, 0.0, v).astype(BF16)
    o1s[...] = by_phase(o1_ref)
    l1s[...] = by_phase(lse1_ref)
    gs[...] = by_phase(gate_ref)

    def residue(r, _):
        phases = [4 * c + r for c in range(4)]
        q = jnp.concatenate([qs[p] for p in phases], axis=0)
        scores = [_dot_nt(q, jnp.concatenate([kp[h][p] for p in phases], axis=0)) for h in range(2)]
        e4, l4, m4, e16, l16, m16 = [], [], [], [], [], []
        for h in range(2):
            s = scores[h]
            for c in range(4):
                raw = [s[c * na:(c + 1) * na, c2 * na:(c2 + 1) * na] for c2 in range(4)]
                blks = [raw[c2] + t4_ref[h, c - c2 + 3] for c2 in range(4)]
                mx = jnp.maximum(jnp.maximum(blks[0], blks[1]), jnp.maximum(blks[2], blks[3]))
                m = jnp.max(mx, axis=-1, keepdims=True)
                eb = [jnp.exp2(b - m) for b in blks]
                l4.append(jnp.sum(eb[0] + eb[1] + eb[2] + eb[3], axis=-1, keepdims=True))
                m4.append(m)
                e4.append(jnp.concatenate([e.astype(BF16) for e in eb], axis=1))
                s16 = raw[c] + t16_ref[h]
                m = jnp.max(s16, axis=-1, keepdims=True)
                e = jnp.exp2(s16 - m)
                l16.append(jnp.sum(e, axis=-1, keepdims=True))
                m16.append(m)
                e16.append(e.astype(BF16))
        pv4 = [_dot(jnp.concatenate(e4[4 * h:4 * h + 4], axis=0),
                    jnp.concatenate([vp[h][p] for p in phases], axis=0)) for h in range(2)]
        pv16 = [[_dot(e16[4 * h + c], vp[h][phases[c]]) for c in range(4)] for h in range(2)]
        for c, p in enumerate(phases):
            o4 = pv4[0][c * na:(c + 1) * na, :] / l4[c] + pv4[1][c * na:(c + 1) * na, :] / l4[4 + c]
            o16 = pv16[0][c] / l16[c] + pv16[1][c] / l16[4 + c]
            lb = jnp.where(lo, m4[c] + jnp.log2(l4[c]), m4[4 + c] + jnp.log2(l4[4 + c]))
            lc = jnp.where(lo, m16[c] + jnp.log2(l16[c]), m16[4 + c] + jnp.log2(l16[4 + c]))
            la = l1s[p]
            mx = jnp.maximum(jnp.maximum(la, lb), lc)
            ea, eb, ec = jnp.exp2(la - mx), jnp.exp2(lb - mx), jnp.exp2(lc - mx)
            o = (ea * o1s[p] + eb * o4 + ec * o16) / (ea + eb + ec)
            res[p] = o * _silu(gs[p])
        return 0

    lax.fori_loop(0, 4, residue, 0)
    o_ref[...] = pltpu.einshape("pal->apl", res[...]).reshape(seq, LANES)


def _dil(proj4, o1, lse1, t_d4, t_d16):
    _, bsz, seq, _ = proj4.shape
    na = seq // PHASES

    def spec(base):
        return pl.BlockSpec((None, None, seq, LANES), lambda b, h: (base + h, b, 0, 0))

    return pl.pallas_call(
        functools.partial(_dil_kernel, seq=seq),
        grid=(bsz, 4),
        in_specs=[spec(CB_DQ), spec(CB_DK), spec(CB_DV), spec(0), spec(0), spec(CB_DGATE),
                  pl.BlockSpec((None, 2, 7, TILE, TILE), lambda b, h: (h, 0, 0, 0, 0)),
                  pl.BlockSpec((None, 2, TILE, TILE), lambda b, h: (h, 0, 0, 0))],
        out_specs=spec(0),
        out_shape=jax.ShapeDtypeStruct((4, bsz, seq, LANES), F32),
        scratch_shapes=[pltpu.VMEM((PHASES, na, LANES), BF16)] * 5 + [pltpu.VMEM((PHASES, na, LANES), F32)] * 4,
        compiler_params=_cparams("arbitrary", "arbitrary"),
        name="dil",
    )(proj4, proj4, proj4, o1, lse1, proj4, t_d4, t_d16)


def _compress_kernel(kv_ref, pos_ref, wlo_ref, whi_ref, w2_ref, o_ref, *, seq):
    na = seq // PHASES
    ph = pltpu.einshape("apl->pal", kv_ref[...].reshape(na, PHASES, LANES))
    r = jnp.concatenate([ph[p] for p in range(PHASES)], axis=1)
    h_lo = _dot((r + pos_ref[0:1, :]).astype(BF16), wlo_ref[...])
    h_hi = _dot((r + pos_ref[1:2, :]).astype(BF16), whi_ref[...])
    h = h_lo + pltpu.roll(h_hi, na - 1, axis=0)
    o_ref[...] = _dot(_silu(h).astype(BF16), w2_ref[...])


def _compress(proj4, pos2, w_lo, w_hi, w2):
    _, bsz, seq, _ = proj4.shape
    na = seq // PHASES
    wide = PHASES * LANES
    full = lambda shape: pl.BlockSpec(shape, lambda b, g: (0,) * len(shape))
    return pl.pallas_call(
        functools.partial(_compress_kernel, seq=seq),
        grid=(bsz, N_KV),
        in_specs=[pl.BlockSpec((None, None, seq, LANES), lambda b, g: (CB_ACMP + g, b, 0, 0)),
                  full((2, wide)), full((wide, 2 * CMP_HIDDEN)), full((wide, 2 * CMP_HIDDEN)),
                  full((2 * CMP_HIDDEN, LANES))],
        out_specs=pl.BlockSpec((None, None, na, LANES), lambda b, g: (b, g, 0, 0)),
        out_shape=jax.ShapeDtypeStruct((bsz, N_KV, na, LANES), F32),
        compiler_params=_cparams("arbitrary", "arbitrary"),
        name="compress",
    )(proj4, pos2, w_lo, w_hi, w2)


def _cmp_kernel(q_ref, kv_ref, bias_ref, ovt_ref, o_ref, feat_ref, *, seq):
    lo = _lane_lo()
    kv = kv_ref[...]
    k_lo = jnp.where(lo, kv, 0.0)
    v_hi = jnp.where(lo, 0.0, kv)
    kp = (k_lo.astype(BF16), pltpu.roll(k_lo, HEAD_DIM, axis=1).astype(BF16))
    vp = (pltpu.roll(v_hi, HEAD_DIM, axis=1).astype(BF16), v_hi.astype(BF16))
    ovt = ovt_ref[...]
    n_slc = ovt.shape[0]
    first = lax.broadcasted_iota(jnp.int32, (n_slc, CMP_ROWS), 0) == 0
    pad_rows = jnp.where(first, -1.0, 0.0).astype(BF16)
    eye = (lax.broadcasted_iota(jnp.int32, (TILE, TILE), 0)
           == lax.broadcasted_iota(jnp.int32, (TILE, TILE), 1)).astype(BF16)

    def qgroup(it, _):
        r0 = pl.multiple_of(it * CMP_ROWS, CMP_ROWS)
        rows = pl.ds(r0, CMP_ROWS)
        qs = [(q_ref[cb, rows, :] * Q_SCALE).astype(BF16) for cb in range(2)]
        scores = [[_dot_nt(qs[cb], kp[hh]) for hh in range(2)] for cb in range(2)]
        probs = [[None, None], [None, None]]
        psum = jnp.zeros((CMP_ROWS, LANES), F32)
        for cb in range(2):
            for hh in range(2):
                s = scores[cb][hh] + bias_ref[2 * cb + hh, rows, :]
                m = jnp.maximum(jnp.max(s, axis=-1, keepdims=True), -1e20)
                e = jnp.exp2(s - m)
                p = e / jnp.maximum(jnp.sum(e, axis=-1, keepdims=True), 1e-30)
                psum = psum + p
                probs[cb][hh] = p.astype(BF16)
        for cb in range(2):
            o_ref[cb, rows, :] = _dot(probs[cb][0], vp[0]) + _dot(probs[cb][1], vp[1])
        p1 = psum.astype(BF16)
        rem = psum - p1.astype(F32)
        p2 = rem.astype(BF16)
        p3 = (rem - p2.astype(F32)).astype(BF16)
        imp = _dot_nt(ovt, p1) + _dot_nt(ovt, p2) + _dot_nt(ovt, p3)
        blk = lax.broadcasted_iota(jnp.int32, (n_slc, CMP_ROWS), 0)
        t = r0 + lax.broadcasted_iota(jnp.int32, (n_slc, CMP_ROWS), 1)
        cur = t // SLC_BLK
        forced = (blk == 0) | (blk == cur) | (blk == cur - 1)
        imp = jnp.where(blk > cur, NEG, jnp.where(forced, FORCE, imp))
        rank = jnp.zeros((n_slc, CMP_ROWS), F32)
        for mth in range(n_slc):
            row = imp[mth:mth + 1, :]
            tie = (blk > mth).astype(F32)
            rank = rank + jnp.where(row > imp, 1.0, jnp.where(row == imp, tie, 0.0))
        unsel = jnp.where(rank < N_SEL, 0.0, -1.0).astype(BF16)
        half = jnp.concatenate([unsel, pad_rows], axis=0)
        feat_t = jnp.concatenate([half, half], axis=0)
        for u in range(CMP_ROWS // TILE):
            piece = _dot_nt(eye, feat_t[:, u * TILE:(u + 1) * TILE])
            feat_ref[pl.ds(r0 + u * TILE, TILE), :] = piece.astype(BF16)
        return 0

    lax.fori_loop(0, seq // CMP_ROWS, qgroup, 0)


def _cmp_select(proj4, kvcmp, t_cmp, ovt):
    _, bsz, seq, _ = proj4.shape
    return pl.pallas_call(
        functools.partial(_cmp_kernel, seq=seq),
        grid=(bsz, N_KV),
        in_specs=[pl.BlockSpec((2, None, seq, LANES), lambda b, g: (CB_AQ // 2 + g, b, 0, 0)),
                  pl.BlockSpec((None, None, TILE, LANES), lambda b, g: (b, g, 0, 0)),
                  pl.BlockSpec((4, seq, LANES), lambda b, g: (g, 0, 0)),
                  pl.BlockSpec(ovt.shape, lambda b, g: (0, 0))],
        out_specs=(pl.BlockSpec((2, None, seq, LANES), lambda b, g: (g, b, 0, 0)),
                   pl.BlockSpec((None, None, seq, LANES), lambda b, g: (b, g, 0, 0))),
        out_shape=(jax.ShapeDtypeStruct((4, bsz, seq, LANES), F32),
                   jax.ShapeDtypeStruct((bsz, N_KV, seq, LANES), BF16)),
        compiler_params=_cparams("arbitrary", "arbitrary"),
        name="cmp_select",
    )(proj4, kvcmp, t_cmp, ovt)


def _slcwin_kernel(q_ref, ks_ref, vs_ref, kw_ref, vw_ref, ocmp_ref, feat_ref, gates_ref, agate_ref,
                   tslc_ref, twin_ref, x_ref, o_ref,
                   ksa0, ksa1, vsa0, vsa1, kwa0, kwa1, vwa0, vwa1, *, seq):
    g = pl.program_id(1)
    lo = _lane_lo()
    ksa, vsa, kwa, vwa = (ksa0, ksa1), (vsa0, vsa1), (kwa0, kwa1), (vwa0, vwa1)
    slc_pad = SLC_CHUNK - TILE
    win_pad = (WIN_TILES - 1) * TILE
    _store_kv_aug(_group_both_halves(g, ks_ref[...]), _group_both_halves(g, vs_ref[...]),
                  ksa, vsa, slc_pad, True)
    _store_kv_aug(_group_both_halves(g, kw_ref[...]), _group_both_halves(g, vw_ref[...]),
                  kwa, vwa, win_pad, False)
    slc_tiles = SLC_CHUNK // TILE

    def qgroup(it, _):
        tiles = [it * SLC_STREAMS + u for u in range(SLC_STREAMS)]
        rows = [pl.ds(pl.multiple_of(i * TILE, TILE), TILE) for i in tiles]
        streams = [([(q_ref[cb, r, :] * Q_SCALE).astype(BF16) for cb in range(2)],
                    feat_ref[r, :], i) for i, r in zip(tiles, rows)]
        n_chunks = (tiles[0] + slc_tiles) // slc_tiles
        res_w = _attend(streams, None, WIN_TILES * TILE, win_pad, kwa, vwa, lambda h, d: twin_ref[h, d])
        res_s = _attend(streams, n_chunks, SLC_CHUNK, slc_pad, ksa, vsa, lambda h, d: tslc_ref[h, d])
        for (_, acc_s), (_, acc_w), r in zip(res_s, res_w, rows):
            sg = jax.nn.sigmoid(gates_ref[r, :])
            sg1 = sg.astype(BF16)
            sg2 = (sg - sg1.astype(F32)).astype(BF16)
            for cb in range(2):
                a0, l0, a1, l1 = _split_acc(acc_s, cb)
                o_slc = jnp.where(lo, a0 / l0, a1 / l1)
                a0, l0, a1, l1 = _split_acc(acc_w, cb)
                o_win = jnp.where(lo, a0 / l0, a1 / l1)
                gts = [_dot(sg1, x_ref[br, cb]) + _dot(sg2, x_ref[br, cb]) for br in range(3)]
                o = gts[0] * ocmp_ref[cb, r, :] + gts[1] * o_slc + gts[2] * o_win
                o_ref[cb, r, :] = o * _silu(agate_ref[cb, r, :])
        return 0

    assert slc_tiles % SLC_STREAMS == 0
    lax.fori_loop(0, seq // (TILE * SLC_STREAMS), qgroup, 0)


def _slcwin(proj4, o_cmp, feat, t_slc, t_win, x_tab):
    _, bsz, seq, _ = proj4.shape
    nt = seq // TILE

    def one(cb):
        return pl.BlockSpec((None, None, seq, LANES), lambda b, g: (cb, b, 0, 0))

    def two(base):
        return pl.BlockSpec((2, None, seq, LANES), lambda b, g: (base // 2 + g, b, 0, 0))

    slc_rows = seq + SLC_CHUNK - TILE
    win_rows = seq + (WIN_TILES - 1) * TILE
    return pl.pallas_call(
        functools.partial(_slcwin_kernel, seq=seq),
        grid=(bsz, N_KV),
        in_specs=[two(CB_AQ), one(CB_AKS), one(CB_AVS), one(CB_AKW), one(CB_AVW),
                  pl.BlockSpec((2, None, seq, LANES), lambda b, g: (g, b, 0, 0)),
                  pl.BlockSpec((None, None, seq, LANES), lambda b, g: (b, g, 0, 0)),
                  one(CB_AGATES), two(CB_AGATE),
                  pl.BlockSpec((4, nt, TILE, TILE), lambda b, g: (g, 0, 0, 0)),
                  pl.BlockSpec((4, WIN_TILES, TILE, TILE), lambda b, g: (g, 0, 0, 0)),
                  pl.BlockSpec((None, 3, 2, TILE, TILE), lambda b, g: (g, 0, 0, 0, 0))],
        out_specs=pl.BlockSpec((2, None, seq, LANES), lambda b, g: (g, b, 0, 0)),
        out_shape=jax.ShapeDtypeStruct((4, bsz, seq, LANES), F32),
        scratch_shapes=[pltpu.VMEM((slc_rows, LANES), BF16)] * 4 + [pltpu.VMEM((win_rows, LANES), BF16)] * 4,
        compiler_params=_cparams("arbitrary", "arbitrary"),
        name="slcwin",
    )(proj4, proj4, proj4, proj4, proj4, o_cmp, feat, proj4, proj4, t_slc, t_win, x_tab)


def _toeplitz_kernel(par_ref, v_ref, o_ref, *, n_tiles, interleave):
    a = lax.broadcasted_iota(jnp.int32, (TILE, TILE), 0)
    b = lax.broadcasted_iota(jnp.int32, (TILE, TILE), 1)
    for t in range(n_tiles):
        x = jnp.broadcast_to(v_ref[t:t + 1, :], (TILE, 2 * TILE))
        y = pltpu.roll(x, 0, 1, stride=1, stride_axis=0)[:, :TILE]
        dist = par_ref[t, 0] + par_ref[t, 1] * (a - b)
        ok = (dist >= par_ref[t, 2]) & (dist <= par_ref[t, 3]) & (b < par_ref[t, 4])
        tile = jnp.where(ok, y * LOG2E, NEG)
        if interleave:
            o_ref[pl.ds(t, TILE, stride=n_tiles), :] = tile
        else:
            o_ref[t] = tile


def _toeplitz(v, params, interleave=False):
    heads, n_tiles, _ = v.shape
    if interleave:
        oshape, ospec = (heads, n_tiles * TILE, TILE), pl.BlockSpec((None, n_tiles * TILE, TILE), lambda h: (h, 0, 0))
    else:
        oshape, ospec = (heads, n_tiles, TILE, TILE), pl.BlockSpec((None, n_tiles, TILE, TILE), lambda h: (h, 0, 0, 0))
    return pl.pallas_call(
        functools.partial(_toeplitz_kernel, n_tiles=n_tiles, interleave=interleave),
        grid=(heads,),
        in_specs=[pl.BlockSpec(memory_space=pltpu.SMEM),
                  pl.BlockSpec((None, n_tiles, 2 * TILE), lambda h: (h, 0, 0))],
        out_specs=ospec,
        out_shape=jax.ShapeDtypeStruct(oshape, F32),
        compiler_params=_cparams("arbitrary"),
        name="toeplitz",
    )(jnp.asarray(params, jnp.int32), v)


def _t5_bucket(d):
    exact = NUM_BUCKETS // 2
    large = exact + (jnp.log(jnp.maximum(d, exact).astype(F32) / exact)
                     / math.log(MAX_DISTANCE / exact) * (NUM_BUCKETS - exact)).astype(jnp.int32)
    return jnp.where(d < exact, d, jnp.minimum(large, NUM_BUCKETS - 1))


def _generator_rows(ext, off, base, mul):
    def run(start):
        if mul == 1:
            seg = ext[:, start:start + TILE]
        else:
            by_phase = ext.reshape(ext.shape[0], -1, mul).transpose(0, 2, 1)
            seg = by_phase[:, start % mul, start // mul:start // mul + TILE]
        return seg[:, ::-1]

    return jnp.concatenate([run(off + base - 127 * mul),
                            run(off + base + mul)], axis=1)


def _bias_tables(rel_bias, seq):
    big = 1 << 30
    onehot = jax.nn.one_hot(_t5_bucket(jnp.arange(seq, dtype=jnp.int32)), NUM_BUCKETS, dtype=F32)
    bias_d = jnp.dot(onehot, rel_bias, precision=lax.Precision.HIGHEST).T
    off = 128 * PHASES + 64
    ext = jnp.pad(bias_d, ((0, 0), (off, off)))
    ba, bb, bd = ext[:8], ext[8:16], ext[16:24]
    nt = seq // TILE

    def plain(e, n, hi):
        v = jnp.stack([_generator_rows(e, off, TILE * d, 1) for d in range(n)], axis=1)
        return _toeplitz(v, [(TILE * d, 1, 0, hi, TILE) for d in range(n)])

    t_slc = plain(ba, nt, big)
    t_win = plain(ba, WIN_TILES, WIN_A - 1)
    t_b = plain(bb, 2, WIN_B - 1)
    t_d1 = plain(bd, 2, DIL_MAXDIST)
    v4 = jnp.stack([_generator_rows(bd, off, 4 * dl, 16) for dl in range(-3, 4)], axis=1)
    t_d4 = _toeplitz(v4, [(dl, 4, 0, DIL_MAXDIST, TILE) for dl in range(-3, 4)])
    v16 = jnp.stack([_generator_rows(bd, off, 0, 16)], axis=1)
    t_d16 = _toeplitz(v16, [(0, 1, 0, big, TILE)])
    n_cmp = (seq - CMP_BLK) // CMP_STRIDE + 1
    vc = jnp.stack([_generator_rows(ba, off, p - (CMP_BLK - 1), CMP_STRIDE) for p in range(PHASES)], axis=1)
    t_cmp = _toeplitz(vc, [(p - (CMP_BLK - 1), CMP_STRIDE, 0, big, n_cmp) for p in range(PHASES)],
                      interleave=True)
    return dict(slc=t_slc, win=t_win, b=t_b, d1=t_d1.reshape(4, 2, 2, TILE, TILE),
                d4=t_d4.reshape(4, 2, 7, TILE, TILE), d16=t_d16.reshape(4, 2, TILE, TILE), cmp=t_cmp)


def _const_tables(seq):
    n_cmp = (seq - CMP_BLK) // CMP_STRIDE + 1
    n_slc = seq // SLC_BLK
    c0 = np.arange(TILE)[None, :] * CMP_STRIDE
    s0 = np.arange(n_slc)[:, None] * SLC_BLK
    ovt = ((c0 < s0 + SLC_BLK) & (c0 + CMP_BLK > s0) & (np.arange(TILE)[None, :] < n_cmp))
    row = np.arange(TILE)[None, None, None, :, None]
    lane = np.arange(TILE)[None, None, None, None, :]
    g = np.arange(N_KV)[:, None, None, None, None]
    br = np.arange(3)[None, :, None, None, None]
    cb = np.arange(2)[None, None, :, None, None]
    x_tab = (row == br * N_HEADS + 4 * g + 2 * cb + lane // HEAD_DIM)
    as_bf = lambda m: jnp.asarray(m.astype(np.float32), BF16)
    return as_bf(ovt), as_bf(x_tab)


W_IN_GATES = 1280
PREP_COLS = 2 * LANES


def _prep_w_in_kernel(wt_ref, o_ref):
    w = wt_ref[0]
    q = HEAD_DIM
    swapped = jnp.concatenate([w[0:q], w[2 * q:3 * q], w[q:2 * q], w[3 * q:4 * q]], axis=0)
    w = jnp.where(pl.program_id(1) == CB_ACMP // 2, swapped, w)
    o_ref[...] = w.T.astype(BF16)


def _prep_w_in(w_in):
    depth, d, _ = w_in.shape
    wt = jnp.swapaxes(w_in, 1, 2)
    n_pairs = (N_CB + 1) // 2
    first_shifted = CB_AGATE // 2

    def src(j):
        start = jnp.where(j < first_shifted, PREP_COLS * j,
                          jnp.where(j < n_pairs - 1, PREP_COLS * j + 24, W_IN_GATES))
        return pl.multiple_of(start, 8)

    return pl.pallas_call(
        _prep_w_in_kernel,
        grid=(depth, n_pairs),
        in_specs=[pl.BlockSpec((pl.Element(1), pl.Element(PREP_COLS), pl.Element(d)),
                               lambda l, j: (l, src(j), 0))],
        out_specs=pl.BlockSpec((None, d, PREP_COLS), lambda l, j: (l, 0, j)),
        out_shape=jax.ShapeDtypeStruct((depth, d, n_pairs * PREP_COLS), BF16),
        compiler_params=_cparams("arbitrary", "arbitrary"),
        name="prep_w_in",
    )(wt)


def _compress_weights(cmp_pos, cmp_w1, cmp_w2):
    half = CMP_BLK // 2
    w1 = cmp_w1.reshape(2, CMP_BLK, HEAD_DIM, CMP_HIDDEN)
    z = jnp.zeros((half, HEAD_DIM, CMP_HIDDEN), F32)

    def stack(part):
        top = jnp.concatenate([w1[0, part], z], axis=-1)
        bot = jnp.concatenate([z, w1[1, part]], axis=-1)
        return jnp.concatenate([top, bot], axis=1).reshape(half * LANES, 2 * CMP_HIDDEN).astype(BF16)

    w_lo, w_hi = stack(slice(0, half)), stack(slice(half, CMP_BLK))
    z2 = jnp.zeros((CMP_HIDDEN, HEAD_DIM), F32)
    w2 = jnp.concatenate([jnp.concatenate([cmp_w2[0], z2], axis=1),
                          jnp.concatenate([z2, cmp_w2[1]], axis=1)], axis=0).astype(BF16)
    pos = jnp.concatenate([cmp_pos[0], cmp_pos[1]], axis=-1)
    pos2 = pos.reshape(2, half * LANES)
    return pos2, w_lo, w_hi, w2


def _layer(x2, bsz, seq, layer, norm_w, w_perm, w_out, conv_w, sinks, cmp_pos, cmp_w1, cmp_w2, final_w, final,
           tabs, consts):
    ovt, x_tab = consts
    proj = _inproj(x2, norm_w, w_perm, layer)
    proj4 = proj.reshape(N_CB, bsz, seq, LANES)
    kvcmp = _compress(proj4, *_compress_weights(cmp_pos, cmp_w1, cmp_w2))
    o_cmp, feat = _cmp_select(proj4, kvcmp, tabs["cmp"], ovt)
    mix_a = _slcwin(proj4, o_cmp, feat, tabs["slc"], tabs["win"], x_tab)
    mix_b, mix_c = _swa_conv(proj4, sinks, tabs["b"], conv_w)
    o1, lse1 = _dil1(proj4, tabs["d1"])
    mix_d = _dil(proj4, o1, lse1, tabs["d4"], tabs["d16"])
    flat = lambda t: t.reshape(4, bsz * seq, LANES)
    return _outproj(flat(mix_a), flat(mix_b), flat(mix_c), flat(mix_d), w_out.astype(BF16), x2,
                    final_w, final)


def kernel(x, norm_w, w_in, w_out, conv_w, sinks, cmp_pos, cmp_w1, cmp_w2, rel_bias, final_norm_w):
    bsz, seq, _ = x.shape
    depth = norm_w.shape[0]
    tabs = _bias_tables(rel_bias, seq)
    consts = _const_tables(seq)
    w_perm = _prep_w_in(w_in)
    x2 = x.reshape(bsz * seq, D_MODEL)
    for layer in range(depth):
        x2 = _layer(x2, bsz, seq, layer, norm_w[layer], w_perm, w_out[layer], conv_w[layer], sinks[layer],
                    cmp_pos[layer], cmp_w1[layer], cmp_w2[layer], final_norm_w, layer == depth - 1,
                    tabs, consts)
    return x2.reshape(bsz, seq, D_MODEL)
```

```python
import functools
import math

import numpy as np
import jax
import jax.numpy as jnp
from jax import lax
from jax.experimental import pallas as pl
from jax.experimental.pallas import tpu as pltpu

F32 = jnp.float32
BF16 = jnp.bfloat16

D_MODEL = 2048
HEAD_DIM = 64
N_HEADS = 8
N_KV = 2
CMP_BLK = 32
CMP_STRIDE = 16
CMP_HIDDEN = 128
SLC_BLK = 64
N_SEL = 8
WIN_A = 512
WIN_B = 128
DIL_MAXDIST = 128
NUM_BUCKETS = 32
MAX_DISTANCE = 2048
RMS_EPS = 1e-6
NEG = -1e30
BIG = 1e30
FORCE = 1e4
D_IN = 7192
PHASES = 16
LOG2E = math.log2(math.e)
Q_SCALE = HEAD_DIM ** -0.5 * LOG2E

LANES = 128
TILE = 128
VMEM_LIMIT = 56 * 1024 * 1024

SEL_LANE = (HEAD_DIM, 0)
PAD_LANE = (HEAD_DIM + 32, 32)
ONE_LANE = (HEAD_DIM, 0)

CB_AQ = 0
CB_ACMP = 4
CB_AKS, CB_AVS, CB_AKW, CB_AVW = 6, 7, 8, 9
CB_AGATE = 10
CB_BQ, CB_BK, CB_BV, CB_BGATE = 14, 18, 19, 20
CB_CB, CB_CC, CB_CH, CB_CGATE = 24, 28, 32, 36
CB_DQ, CB_DK, CB_DV, CB_DGATE = 40, 44, 48, 52
CB_AGATES = 56
N_CB = 57
CB_PER_STEP = 19

SLC_CHUNK = 4 * TILE
WIN_TILES = 5
SWA_STREAMS = 8
DIL1_STREAMS = 8
SLC_STREAMS = 4
CMP_ROWS = 8 * TILE
CONV_ROWS = 2 * TILE


def _cparams(*sem):
    return pltpu.CompilerParams(dimension_semantics=sem, vmem_limit_bytes=VMEM_LIMIT)


def _lane():
    return lax.broadcasted_iota(jnp.int32, (1, LANES), 1)


def _lane_lo():
    return _lane() < HEAD_DIM


def _dot_nt(a, b):
    return lax.dot_general(a, b, (((1,), (1,)), ((), ())), preferred_element_type=F32)


def _dot(a, b):
    return jnp.dot(a, b, preferred_element_type=F32)


def _silu(x):
    return x * jax.nn.sigmoid(x)


def _inproj_kernel(x_ref, nw_ref, w_ref, o_ref):
    x = x_ref[...]
    y = x * lax.rsqrt(jnp.mean(x * x, axis=-1, keepdims=True) + RMS_EPS)
    xn = (y * nw_ref[...]).astype(BF16)
    acc = _dot(xn, w_ref[...])
    for k in range(CB_PER_STEP):
        o_ref[k] = acc[:, k * LANES:(k + 1) * LANES]


def _inproj(x2, norm_w, w_perm, layer, tm=512):
    m = x2.shape[0]
    n_steps = N_CB // CB_PER_STEP
    return pl.pallas_call(
        _inproj_kernel,
        grid=(n_steps, m // tm),
        in_specs=[
            pl.BlockSpec((tm, D_MODEL), lambda j, i: (i, 0)),
            pl.BlockSpec((1, D_MODEL), lambda j, i: (0, 0)),
            pl.BlockSpec((None, D_MODEL, CB_PER_STEP * LANES), lambda j, i: (layer, 0, j)),
        ],
        out_specs=pl.BlockSpec((CB_PER_STEP, tm, LANES), lambda j, i: (j, i, 0)),
        out_shape=jax.ShapeDtypeStruct((N_CB, m, LANES), F32),
        compiler_params=_cparams("arbitrary", "arbitrary"),
        name="inproj",
    )(x2, norm_w.reshape(1, D_MODEL), w_perm)


def _outproj_kernel(a_ref, b_ref, c_ref, d_ref, w_ref, x_ref, fw_ref, o_ref, *, final):
    mix = jnp.concatenate([r[k] for r in (a_ref, b_ref, c_ref, d_ref) for k in range(4)], axis=1)
    y = x_ref[...] + _dot(mix.astype(BF16), w_ref[...])
    if final:
        y = y * lax.rsqrt(jnp.mean(y * y, axis=-1, keepdims=True) + RMS_EPS) * fw_ref[...]
    o_ref[...] = y


def _outproj(mix_a, mix_b, mix_c, mix_d, w_out_bf, x2, final_w, final, tm=512):
    m = x2.shape[0]
    mspec = pl.BlockSpec((4, tm, LANES), lambda i: (0, i, 0))
    return pl.pallas_call(
        functools.partial(_outproj_kernel, final=final),
        grid=(m // tm,),
        in_specs=[mspec, mspec, mspec, mspec,
                  pl.BlockSpec((D_MODEL, D_MODEL), lambda i: (0, 0)),
                  pl.BlockSpec((tm, D_MODEL), lambda i: (i, 0)),
                  pl.BlockSpec((1, D_MODEL), lambda i: (0, 0))],
        out_specs=pl.BlockSpec((tm, D_MODEL), lambda i: (i, 0)),
        out_shape=jax.ShapeDtypeStruct((m, D_MODEL), F32),
        compiler_params=_cparams("arbitrary"),
        name="outproj",
    )(mix_a, mix_b, mix_c, mix_d, w_out_bf, x2, final_w.reshape(1, D_MODEL))


def _pad_feat():
    lane = _lane()
    row = jnp.where((lane == PAD_LANE[0]) | (lane == PAD_LANE[1]), -1.0, 0.0)
    return jnp.broadcast_to(row, (TILE, LANES)).astype(BF16)


def _store_kv_aug(k, v, ka, va, pad_rows, with_blocks):
    lo = _lane_lo()
    lane = _lane()
    seq = k.shape[0]
    blk = lax.broadcasted_iota(jnp.int32, (seq, 1), 0) // SLC_BLK
    for h in range(2):
        own = lo if h == 0 else jnp.logical_not(lo)
        kfeat = jnp.where(lane - SEL_LANE[h] == blk, BIG, 0.0) if with_blocks else 0.0
        vfeat = jnp.where(lane == ONE_LANE[h], 1.0, 0.0)
        ka[h][pad_rows:pad_rows + seq, :] = jnp.where(own, k, kfeat).astype(BF16)
        va[h][pad_rows:pad_rows + seq, :] = jnp.where(own, v, vfeat).astype(BF16)
        if pad_rows:
            flag = jnp.where(lane == PAD_LANE[h], BIG, 0.0)
            ka[h][0:pad_rows, :] = jnp.broadcast_to(flag, (pad_rows, LANES)).astype(BF16)
            va[h][0:pad_rows, :] = jnp.zeros((pad_rows, LANES), BF16)


def _group_both_halves(g, x):
    own = jnp.where(_lane_lo() == (g == 0), x, 0.0)
    return own + pltpu.roll(own, HEAD_DIM, axis=1)


def _attend(streams, n_chunks, kc, pad_rows, ka, va, bias_fn, head=False):
    lo = _lane_lo()
    ncb = len(streams[0][0])
    nt = kc // TILE
    qas = [(jnp.concatenate([jnp.where(lo, q, qfeat) for q in q_list], axis=0),
            jnp.concatenate([jnp.where(lo, qfeat, q) for q in q_list], axis=0))
           for q_list, qfeat, _ in streams]

    def chunk_windows(k):
        return [(pl.multiple_of(pad_rows + (i + 1) * TILE - (k + 1) * kc, TILE), nt,
                 lambda jt: k * nt + (nt - 1 - jt)) for _, _, i in streams]

    def step(windows, carry):
        scores = [[_dot_nt(qa[h], ka[h][pl.ds(start, n * TILE), :]) for h in range(2)]
                  for qa, (start, n, _) in zip(qas, windows)]
        probs, new_ms, alphas = [], [], []
        for si, (ms, _) in enumerate(carry):
            _, n, dist = windows[si]
            ms_s = [None] * (2 * ncb)
            p_s, a_s = [], []
            for h in range(2):
                s = scores[si][h]
                ps, al = [], []
                for cb in range(ncb):
                    blks = [s[cb * TILE:(cb + 1) * TILE, jt * TILE:(jt + 1) * TILE]
                            + bias_fn(2 * cb + h, dist(jt)) for jt in range(n)]
                    mx = blks[0]
                    for b in blks[1:]:
                        mx = jnp.maximum(mx, b)
                    m_old = ms[2 * cb + h]
                    m_new = jnp.maximum(m_old, jnp.max(mx, axis=-1, keepdims=True))
                    al.append(jnp.broadcast_to(jnp.exp2(m_old - m_new), (TILE, LANES)))
                    ps.append(jnp.concatenate([jnp.exp2(b - m_new).astype(BF16) for b in blks], axis=1))
                    ms_s[2 * cb + h] = m_new
                p_s.append(jnp.concatenate(ps, axis=0))
                a_s.append(jnp.concatenate(al, axis=0))
            probs.append(p_s)
            alphas.append(a_s)
            new_ms.append(tuple(ms_s))
        out = []
        for si, (_, accs) in enumerate(carry):
            start, n, _ = windows[si]
            new_accs = tuple(accs[h] * alphas[si][h] + _dot(probs[si][h], va[h][pl.ds(start, n * TILE), :])
                             for h in range(2))
            out.append((new_ms[si], new_accs))
        return tuple(out)

    init = tuple((tuple(jnp.full((TILE, 1), NEG, F32) for _ in range(2 * ncb)),
                  tuple(jnp.zeros((ncb * TILE, LANES), F32) for _ in range(2))) for _ in streams)
    if n_chunks is None:
        return step(chunk_windows(0), init)
    if not head:
        return lax.fori_loop(0, n_chunks, lambda k, carry: step(chunk_windows(k), carry), init)

    assert len(streams) == nt
    carry = lax.fori_loop(0, n_chunks - 1, lambda k, carry: step(chunk_windows(k), carry), init)
    head_windows = [(pad_rows, u + 1, lambda jt, i=i: i - jt) for u, (_, _, i) in enumerate(streams)]
    return step(head_windows, carry)


def _split_acc(accs, cb):
    a0 = accs[0][cb * TILE:(cb + 1) * TILE, :]
    a1 = accs[1][cb * TILE:(cb + 1) * TILE, :]
    return a0, a0[:, ONE_LANE[0]:ONE_LANE[0] + 1], a1, a1[:, ONE_LANE[1]:ONE_LANE[1] + 1]


def _swa_conv_kernel(sink_ref, q_ref, k_ref, v_ref, gate_ref, bias_ref, cw_ref, cbb_ref, cc_ref, ch_ref, cg_ref,
                     o_ref, oc_ref, ka0, ka1, va0, va1, pad_ref, *, seq):
    g = pl.program_id(1)
    lo = _lane_lo()
    pad_ref[0:8, :] = jnp.zeros((8, LANES), F32)
    for cb in range(2):
        w = cw_ref[:, cb * LANES:(cb + 1) * LANES]
        for r in range(0, seq, CONV_ROWS):
            pad_ref[8 + r:8 + r + CONV_ROWS, :] = (cc_ref[cb, r:r + CONV_ROWS, :]
                                                   * ch_ref[cb, r:r + CONV_ROWS, :])
        for r in range(0, seq, CONV_ROWS):
            y = (w[0:1, :] * pad_ref[6 + r:6 + r + CONV_ROWS, :]
                 + w[1:2, :] * pad_ref[7 + r:7 + r + CONV_ROWS, :]
                 + w[2:3, :] * pad_ref[8 + r:8 + r + CONV_ROWS, :])
            oc_ref[cb, r:r + CONV_ROWS, :] = (cbb_ref[cb, r:r + CONV_ROWS, :] * y
                                              * _silu(cg_ref[cb, r:r + CONV_ROWS, :]))

    ka, va = (ka0, ka1), (va0, va1)
    _store_kv_aug(_group_both_halves(g, k_ref[...]), _group_both_halves(g, v_ref[...]), ka, va, TILE, False)
    qfeat = _pad_feat()

    def qgroup(it, _):
        tiles = [it * SWA_STREAMS + u for u in range(SWA_STREAMS)]
        rows = [pl.ds(pl.multiple_of(i * TILE, TILE), TILE) for i in tiles]
        streams = [([(q_ref[cb, r, :] * Q_SCALE).astype(BF16) for cb in range(2)], qfeat, i)
                   for i, r in zip(tiles, rows)]
        res = _attend(streams, None, 2 * TILE, TILE, ka, va, lambda h, d: bias_ref[h, d])
        for (ms, accs), r in zip(res, rows):
            for cb in range(2):
                a0, l0, a1, l1 = _split_acc(accs, cb)
                outs = []
                for h, (a, l) in enumerate(((a0, l0), (a1, l1))):
                    m = ms[2 * cb + h]
                    sink = sink_ref[g * 4 + 2 * cb + h] * LOG2E
                    mt = jnp.maximum(m, sink)
                    scale = jnp.exp2(m - mt)
                    outs.append(a * (scale / (l * scale + jnp.exp2(sink - mt))))
                o = jnp.where(lo, outs[0], outs[1])
                o_ref[cb, r, :] = o * _silu(gate_ref[cb, r, :])
        return 0

    lax.fori_loop(0, seq // (TILE * SWA_STREAMS), qgroup, 0)


def _swa_conv(proj4, sinks, t_b, conv_w):
    _, bsz, seq, _ = proj4.shape
    cw = jnp.pad(conv_w, ((0, 5), (0, 0)))

    def two(base):
        return pl.BlockSpec((2, None, seq, LANES), lambda b, g: (base // 2 + g, b, 0, 0))

    def one(cb):
        return pl.BlockSpec((None, None, seq, LANES), lambda b, g: (cb, b, 0, 0))

    ospec = pl.BlockSpec((2, None, seq, LANES), lambda b, g: (g, b, 0, 0))
    oshape = jax.ShapeDtypeStruct((4, bsz, seq, LANES), F32)
    return pl.pallas_call(
        functools.partial(_swa_conv_kernel, seq=seq),
        grid=(bsz, N_KV),
        in_specs=[
            pl.BlockSpec(memory_space=pltpu.SMEM),
            two(CB_BQ), one(CB_BK), one(CB_BV), two(CB_BGATE),
            pl.BlockSpec((4, 2, TILE, TILE), lambda b, g: (g, 0, 0, 0)),
            pl.BlockSpec((8, 2 * LANES), lambda b, g: (0, g)),
            two(CB_CB), two(CB_CC), two(CB_CH), two(CB_CGATE),
        ],
        out_specs=(ospec, ospec),
        out_shape=(oshape, oshape),
        scratch_shapes=[pltpu.VMEM((seq + TILE, LANES), BF16)] * 4 + [pltpu.VMEM((seq + 8, LANES), F32)],
        compiler_params=_cparams("arbitrary", "arbitrary"),
        name="swa_conv",
    )(sinks, proj4, proj4, proj4, proj4, t_b, cw, proj4, proj4, proj4, proj4)


def _dil1_kernel(q_ref, k_ref, v_ref, bias_ref, o_ref, lse_ref, ka0, ka1, va0, va1, *, seq):
    lo = _lane_lo()
    ka, va = (ka0, ka1), (va0, va1)
    _store_kv_aug(k_ref[...], v_ref[...], ka, va, TILE, False)
    qfeat = _pad_feat()

    def qgroup(it, _):
        tiles = [it * DIL1_STREAMS + u for u in range(DIL1_STREAMS)]
        rows = [pl.ds(pl.multiple_of(i * TILE, TILE), TILE) for i in tiles]
        streams = [([(q_ref[r, :] * Q_SCALE).astype(BF16)], qfeat, i) for i, r in zip(tiles, rows)]
        res = _attend(streams, None, 2 * TILE, TILE, ka, va, lambda h, d: bias_ref[h, d])
        for (ms, accs), r in zip(res, rows):
            a0, l0, a1, l1 = _split_acc(accs, 0)
            o_ref[r, :] = jnp.where(lo, a0 / l0, a1 / l1)
            lse_ref[r, :] = jnp.where(lo, ms[0] + jnp.log2(l0), ms[1] + jnp.log2(l1))
        return 0

    lax.fori_loop(0, seq // (TILE * DIL1_STREAMS), qgroup, 0)


def _dil1(proj4, t_d1):
    _, bsz, seq, _ = proj4.shape

    def spec(base):
        return pl.BlockSpec((None, None, seq, LANES), lambda b, h: (base + h, b, 0, 0))

    ospec = pl.BlockSpec((None, None, seq, LANES), lambda b, h: (h, b, 0, 0))
    oshape = jax.ShapeDtypeStruct((4, bsz, seq, LANES), F32)
    return pl.pallas_call(
        functools.partial(_dil1_kernel, seq=seq),
        grid=(bsz, 4),
        in_specs=[spec(CB_DQ), spec(CB_DK), spec(CB_DV),
                  pl.BlockSpec((None, 2, 2, TILE, TILE), lambda b, h: (h, 0, 0, 0, 0))],
        out_specs=(ospec, ospec),
        out_shape=(oshape, oshape),
        scratch_shapes=[pltpu.VMEM((seq + TILE, LANES), BF16)] * 4,
        compiler_params=_cparams("arbitrary", "arbitrary"),
        name="dil1",
    )(proj4, proj4, proj4, t_d1)


def _dil_kernel(q_ref, k_ref, v_ref, o1_ref, lse1_ref, gate_ref, t4_ref, t16_ref, o_ref,
                qs, kp0, kp1, vp0, vp1, o1s, l1s, gs, res, *, seq):
    lo = _lane_lo()
    na = seq // PHASES
    kp, vp = (kp0, kp1), (vp0, vp1)

    def by_phase(ref):
        return pltpu.einshape("apl->pal", ref[...].reshape(na, PHASES, LANES))

    qs[...] = (by_phase(q_ref) * Q_SCALE).astype(BF16)
    k = by_phase(k_ref)
    kp0[...] = jnp.where(lo, k, 0.0).astype(BF16)
    kp1[...] = jnp.where(lo, 0.0, k).astype(BF16)
    v = by_phase(v_ref)
    vp0[...] = jnp.where(lo, v, 0.0).astype(BF16)
    vp1[...] = jnp.where(lo, 0.0, v).astype(BF16)
    o1s[...] = by_phase(o1_ref)
    l1s[...] = by_phase(lse1_ref)
    gs[...] = by_phase(gate_ref)

    def residue(r, _):
        phases = [4 * c + r for c in range(4)]
        q = jnp.concatenate([qs[p] for p in phases], axis=0)
        scores = [_dot_nt(q, jnp.concatenate([kp[h][p] for p in phases], axis=0)) for h in range(2)]
        e4, l4, m4, e16, l16, m16 = [], [], [], [], [], []
        for h in range(2):
            s = scores[h]
            for c in range(4):
                raw = [s[c * na:(c + 1) * na, c2 * na:(c2 + 1) * na] for c2 in range(4)]
                blks = [raw[c2] + t4_ref[h, c - c2 + 3] for c2 in range(4)]
                mx = jnp.maximum(jnp.maximum(blks[0], blks[1]), jnp.maximum(blks[2], blks[3]))
                m = jnp.max(mx, axis=-1, keepdims=True)
                eb = [jnp.exp2(b - m) for b in blks]
                l4.append(jnp.sum(eb[0] + eb[1] + eb[2] + eb[3], axis=-1, keepdims=True))
                m4.append(m)
                e4.append(jnp.concatenate([e.astype(BF16) for e in eb], axis=1))
                s16 = raw[c] + t16_ref[h]
                m = jnp.max(s16, axis=-1, keepdims=True)
                e = jnp.exp2(s16 - m)
                l16.append(jnp.sum(e, axis=-1, keepdims=True))
                m16.append(m)
                e16.append(e.astype(BF16))
        pv4 = [_dot(jnp.concatenate(e4[4 * h:4 * h + 4], axis=0),
                    jnp.concatenate([vp[h][p] for p in phases], axis=0)) for h in range(2)]
        pv16 = [[_dot(e16[4 * h + c], vp[h][phases[c]]) for c in range(4)] for h in range(2)]
        for c, p in enumerate(phases):
            o4 = pv4[0][c * na:(c + 1) * na, :] / l4[c] + pv4[1][c * na:(c + 1) * na, :] / l4[4 + c]
            o16 = pv16[0][c] / l16[c] + pv16[1][c] / l16[4 + c]
            lb = jnp.where(lo, m4[c] + jnp.log2(l4[c]), m4[4 + c] + jnp.log2(l4[4 + c]))
            lc = jnp.where(lo, m16[c] + jnp.log2(l16[c]), m16[4 + c] + jnp.log2(l16[4 + c]))
            la = l1s[p]
            mx = jnp.maximum(jnp.maximum(la, lb), lc)
            ea, eb, ec = jnp.exp2(la - mx), jnp.exp2(lb - mx), jnp.exp2(lc - mx)
            o = (ea * o1s[p] + eb * o4 + ec * o16) / (ea + eb + ec)
            res[p] = o * _silu(gs[p])
        return 0

    lax.fori_loop(0, 4, residue, 0)
    o_ref[...] = pltpu.einshape("pal->apl", res[...]).reshape(seq, LANES)


def _dil(proj4, o1, lse1, t_d4, t_d16):
    _, bsz, seq, _ = proj4.shape
    na = seq // PHASES

    def spec(base):
        return pl.BlockSpec((None, None, seq, LANES), lambda b, h: (base + h, b, 0, 0))

    return pl.pallas_call(
        functools.partial(_dil_kernel, seq=seq),
        grid=(bsz, 4),
        in_specs=[spec(CB_DQ), spec(CB_DK), spec(CB_DV), spec(0), spec(0), spec(CB_DGATE),
                  pl.BlockSpec((None, 2, 7, TILE, TILE), lambda b, h: (h, 0, 0, 0, 0)),
                  pl.BlockSpec((None, 2, TILE, TILE), lambda b, h: (h, 0, 0, 0))],
        out_specs=spec(0),
        out_shape=jax.ShapeDtypeStruct((4, bsz, seq, LANES), F32),
        scratch_shapes=[pltpu.VMEM((PHASES, na, LANES), BF16)] * 5 + [pltpu.VMEM((PHASES, na, LANES), F32)] * 4,
        compiler_params=_cparams("arbitrary", "arbitrary"),
        name="dil",
    )(proj4, proj4, proj4, o1, lse1, proj4, t_d4, t_d16)


def _compress_kernel(kv_ref, pos_ref, wlo_ref, whi_ref, w2_ref, o_ref, *, seq):
    na = seq // PHASES
    ph = pltpu.einshape("apl->pal", kv_ref[...].reshape(na, PHASES, LANES))
    r = jnp.concatenate([ph[p] for p in range(PHASES)], axis=1)
    h_lo = _dot((r + pos_ref[0:1, :]).astype(BF16), wlo_ref[...])
    h_hi = _dot((r + pos_ref[1:2, :]).astype(BF16), whi_ref[...])
    h = h_lo + pltpu.roll(h_hi, na - 1, axis=0)
    o_ref[...] = _dot(_silu(h).astype(BF16), w2_ref[...])


def _compress(proj4, pos2, w_lo, w_hi, w2):
    _, bsz, seq, _ = proj4.shape
    na = seq // PHASES
    wide = PHASES * LANES
    full = lambda shape: pl.BlockSpec(shape, lambda b, g: (0,) * len(shape))
    return pl.pallas_call(
        functools.partial(_compress_kernel, seq=seq),
        grid=(bsz, N_KV),
        in_specs=[pl.BlockSpec((None, None, seq, LANES), lambda b, g: (CB_ACMP + g, b, 0, 0)),
                  full((2, wide)), full((wide, 2 * CMP_HIDDEN)), full((wide, 2 * CMP_HIDDEN)),
                  full((2 * CMP_HIDDEN, LANES))],
        out_specs=pl.BlockSpec((None, None, na, LANES), lambda b, g: (b, g, 0, 0)),
        out_shape=jax.ShapeDtypeStruct((bsz, N_KV, na, LANES), F32),
        compiler_params=_cparams("arbitrary", "arbitrary"),
        name="compress",
    )(proj4, pos2, w_lo, w_hi, w2)


def _cmp_kernel(q_ref, kv_ref, bias_ref, ovt_ref, o_ref, feat_ref, *, seq):
    lo = _lane_lo()
    kv = kv_ref[...]
    k_lo = jnp.where(lo, kv, 0.0)
    v_hi = jnp.where(lo, 0.0, kv)
    kp = (k_lo.astype(BF16), pltpu.roll(k_lo, HEAD_DIM, axis=1).astype(BF16))
    vp = (pltpu.roll(v_hi, HEAD_DIM, axis=1).astype(BF16), v_hi.astype(BF16))
    ovt = ovt_ref[...]
    n_slc = ovt.shape[0]
    first = lax.broadcasted_iota(jnp.int32, (n_slc, CMP_ROWS), 0) == 0
    pad_rows = jnp.where(first, -1.0, 0.0).astype(BF16)
    eye = (lax.broadcasted_iota(jnp.int32, (TILE, TILE), 0)
           == lax.broadcasted_iota(jnp.int32, (TILE, TILE), 1)).astype(BF16)

    def qgroup(it, _):
        r0 = pl.multiple_of(it * CMP_ROWS, CMP_ROWS)
        rows = pl.ds(r0, CMP_ROWS)
        qs = [(q_ref[cb, rows, :] * Q_SCALE).astype(BF16) for cb in range(2)]
        scores = [[_dot_nt(qs[cb], kp[hh]) for hh in range(2)] for cb in range(2)]
        probs = [[None, None], [None, None]]
        psum = jnp.zeros((CMP_ROWS, LANES), F32)
        for cb in range(2):
            for hh in range(2):
                s = scores[cb][hh] + bias_ref[2 * cb + hh, rows, :]
                m = jnp.maximum(jnp.max(s, axis=-1, keepdims=True), -1e20)
                e = jnp.exp2(s - m)
                p = e / jnp.maximum(jnp.sum(e, axis=-1, keepdims=True), 1e-30)
                psum = psum + p
                probs[cb][hh] = p.astype(BF16)
        for cb in range(2):
            o_ref[cb, rows, :] = _dot(probs[cb][0], vp[0]) + _dot(probs[cb][1], vp[1])
        p1 = psum.astype(BF16)
        rem = psum - p1.astype(F32)
        p2 = rem.astype(BF16)
        p3 = (rem - p2.astype(F32)).astype(BF16)
        imp = _dot_nt(ovt, p1) + _dot_nt(ovt, p2) + _dot_nt(ovt, p3)
        blk = lax.broadcasted_iota(jnp.int32, (n_slc, CMP_ROWS), 0)
        t = r0 + lax.broadcasted_iota(jnp.int32, (n_slc, CMP_ROWS), 1)
        cur = t // SLC_BLK
        forced = (blk == 0) | (blk == cur) | (blk == cur - 1)
        imp = jnp.where(blk > cur, NEG, jnp.where(forced, FORCE, imp))
        rank = jnp.zeros((n_slc, CMP_ROWS), F32)
        for mth in range(n_slc):
            row = imp[mth:mth + 1, :]
            tie = (blk > mth).astype(F32)
            rank = rank + jnp.where(row > imp, 1.0, jnp.where(row == imp, tie, 0.0))
        unsel = jnp.where(rank < N_SEL, 0.0, -1.0).astype(BF16)
        half = jnp.concatenate([unsel, pad_rows], axis=0)
        feat_t = jnp.concatenate([half, half], axis=0)
        for u in range(CMP_ROWS // TILE):
            piece = _dot_nt(eye, feat_t[:, u * TILE:(u + 1) * TILE])
            feat_ref[pl.ds(r0 + u * TILE, TILE), :] = piece.astype(BF16)
        return 0

    lax.fori_loop(0, seq // CMP_ROWS, qgroup, 0)


def _cmp_select(proj4, kvcmp, t_cmp, ovt):
    _, bsz, seq, _ = proj4.shape
    return pl.pallas_call(
        functools.partial(_cmp_kernel, seq=seq),
        grid=(bsz, N_KV),
        in_specs=[pl.BlockSpec((2, None, seq, LANES), lambda b, g: (CB_AQ // 2 + g, b, 0, 0)),
                  pl.BlockSpec((None, None, TILE, LANES), lambda b, g: (b, g, 0, 0)),
                  pl.BlockSpec((4, seq, LANES), lambda b, g: (g, 0, 0)),
                  pl.BlockSpec(ovt.shape, lambda b, g: (0, 0))],
        out_specs=(pl.BlockSpec((2, None, seq, LANES), lambda b, g: (g, b, 0, 0)),
                   pl.BlockSpec((None, None, seq, LANES), lambda b, g: (b, g, 0, 0))),
        out_shape=(jax.ShapeDtypeStruct((4, bsz, seq, LANES), F32),
                   jax.ShapeDtypeStruct((bsz, N_KV, seq, LANES), BF16)),
        compiler_params=_cparams("arbitrary", "arbitrary"),
        name="cmp_select",
    )(proj4, kvcmp, t_cmp, ovt)


def _slcwin_kernel(q_ref, ks_ref, vs_ref, kw_ref, vw_ref, ocmp_ref, feat_ref, gates_ref, agate_ref,
                   tslc_ref, twin_ref, x_ref, o_ref,
                   ksa0, ksa1, vsa0, vsa1, kwa0, kwa1, vwa0, vwa1, *, seq):
    g = pl.program_id(1)
    lo = _lane_lo()
    ksa, vsa, kwa, vwa = (ksa0, ksa1), (vsa0, vsa1), (kwa0, kwa1), (vwa0, vwa1)
    win_pad = (WIN_TILES - 1) * TILE
    _store_kv_aug(_group_both_halves(g, ks_ref[...]), _group_both_halves(g, vs_ref[...]),
                  ksa, vsa, 0, True)
    _store_kv_aug(_group_both_halves(g, kw_ref[...]), _group_both_halves(g, vw_ref[...]),
                  kwa, vwa, win_pad, False)
    slc_tiles = SLC_CHUNK // TILE

    def qgroup(it, _):
        tiles = [it * SLC_STREAMS + u for u in range(SLC_STREAMS)]
        rows = [pl.ds(pl.multiple_of(i * TILE, TILE), TILE) for i in tiles]
        streams = [([(q_ref[cb, r, :] * Q_SCALE).astype(BF16) for cb in range(2)],
                    feat_ref[r, :], i) for i, r in zip(tiles, rows)]
        n_chunks = (tiles[0] + slc_tiles) // slc_tiles
        res_w = _attend(streams, None, WIN_TILES * TILE, win_pad, kwa, vwa, lambda h, d: twin_ref[h, d])
        res_s = _attend(streams, n_chunks, SLC_CHUNK, 0, ksa, vsa, lambda h, d: tslc_ref[h, d], head=True)
        for (_, acc_s), (_, acc_w), r in zip(res_s, res_w, rows):
            sg = jax.nn.sigmoid(gates_ref[r, :])
            sg1 = sg.astype(BF16)
            sg2 = (sg - sg1.astype(F32)).astype(BF16)
            for cb in range(2):
                a0, l0, a1, l1 = _split_acc(acc_s, cb)
                o_slc = jnp.where(lo, a0 / l0, a1 / l1)
                a0, l0, a1, l1 = _split_acc(acc_w, cb)
                o_win = jnp.where(lo, a0 / l0, a1 / l1)
                gts = [_dot(sg1, x_ref[br, cb]) + _dot(sg2, x_ref[br, cb]) for br in range(3)]
                o = gts[0] * ocmp_ref[cb, r, :] + gts[1] * o_slc + gts[2] * o_win
                o_ref[cb, r, :] = o * _silu(agate_ref[cb, r, :])
        return 0

    assert slc_tiles % SLC_STREAMS == 0
    lax.fori_loop(0, seq // (TILE * SLC_STREAMS), qgroup, 0)


def _slcwin(proj4, o_cmp, feat, t_slc, t_win, x_tab):
    _, bsz, seq, _ = proj4.shape
    nt = seq // TILE

    def one(cb):
        return pl.BlockSpec((None, None, seq, LANES), lambda b, g: (cb, b, 0, 0))

    def two(base):
        return pl.BlockSpec((2, None, seq, LANES), lambda b, g: (base // 2 + g, b, 0, 0))

    slc_rows = seq
    win_rows = seq + (WIN_TILES - 1) * TILE
    return pl.pallas_call(
        functools.partial(_slcwin_kernel, seq=seq),
        grid=(bsz, N_KV),
        in_specs=[two(CB_AQ), one(CB_AKS), one(CB_AVS), one(CB_AKW), one(CB_AVW),
                  pl.BlockSpec((2, None, seq, LANES), lambda b, g: (g, b, 0, 0)),
                  pl.BlockSpec((None, None, seq, LANES), lambda b, g: (b, g, 0, 0)),
                  one(CB_AGATES), two(CB_AGATE),
                  pl.BlockSpec((4, nt, TILE, TILE), lambda b, g: (g, 0, 0, 0)),
                  pl.BlockSpec((4, WIN_TILES, TILE, TILE), lambda b, g: (g, 0, 0, 0)),
                  pl.BlockSpec((None, 3, 2, TILE, TILE), lambda b, g: (g, 0, 0, 0, 0))],
        out_specs=pl.BlockSpec((2, None, seq, LANES), lambda b, g: (g, b, 0, 0)),
        out_shape=jax.ShapeDtypeStruct((4, bsz, seq, LANES), F32),
        scratch_shapes=[pltpu.VMEM((slc_rows, LANES), BF16)] * 4 + [pltpu.VMEM((win_rows, LANES), BF16)] * 4,
        compiler_params=_cparams("arbitrary", "arbitrary"),
        name="slcwin",
    )(proj4, proj4, proj4, proj4, proj4, o_cmp, feat, proj4, proj4, t_slc, t_win, x_tab)


def _toeplitz_kernel(par_ref, v_ref, o_ref, *, n_tiles, interleave):
    a = lax.broadcasted_iota(jnp.int32, (TILE, TILE), 0)
    b = lax.broadcasted_iota(jnp.int32, (TILE, TILE), 1)
    for t in range(n_tiles):
        x = jnp.broadcast_to(v_ref[t:t + 1, :], (TILE, 2 * TILE))
        y = pltpu.roll(x, 0, 1, stride=1, stride_axis=0)[:, :TILE]
        dist = par_ref[t, 0] + par_ref[t, 1] * (a - b)
        ok = (dist >= par_ref[t, 2]) & (dist <= par_ref[t, 3]) & (b < par_ref[t, 4])
        tile = jnp.where(ok, y * LOG2E, NEG)
        if interleave:
            o_ref[pl.ds(t, TILE, stride=n_tiles), :] = tile
        else:
            o_ref[t] = tile


def _toeplitz(v, params, interleave=False):
    heads, n_tiles, _ = v.shape
    if interleave:
        oshape, ospec = (heads, n_tiles * TILE, TILE), pl.BlockSpec((None, n_tiles * TILE, TILE), lambda h: (h, 0, 0))
    else:
        oshape, ospec = (heads, n_tiles, TILE, TILE), pl.BlockSpec((None, n_tiles, TILE, TILE), lambda h: (h, 0, 0, 0))
    return pl.pallas_call(
        functools.partial(_toeplitz_kernel, n_tiles=n_tiles, interleave=interleave),
        grid=(heads,),
        in_specs=[pl.BlockSpec(memory_space=pltpu.SMEM),
                  pl.BlockSpec((None, n_tiles, 2 * TILE), lambda h: (h, 0, 0))],
        out_specs=ospec,
        out_shape=jax.ShapeDtypeStruct(oshape, F32),
        compiler_params=_cparams("arbitrary"),
        name="toeplitz",
    )(jnp.asarray(params, jnp.int32), v)


def _t5_bucket(d):
    exact = NUM_BUCKETS // 2
    large = exact + (jnp.log(jnp.maximum(d, exact).astype(F32) / exact)
                     / math.log(MAX_DISTANCE / exact) * (NUM_BUCKETS - exact)).astype(jnp.int32)
    return jnp.where(d < exact, d, jnp.minimum(large, NUM_BUCKETS - 1))


def _generator_rows(ext, off, base, mul):
    def run(start):
        if mul == 1:
            seg = ext[:, start:start + TILE]
        else:
            by_phase = ext.reshape(ext.shape[0], -1, mul).transpose(0, 2, 1)
            seg = by_phase[:, start % mul, start // mul:start // mul + TILE]
        return seg[:, ::-1]

    return jnp.concatenate([run(off + base - 127 * mul),
                            run(off + base + mul)], axis=1)


def _bias_tables(rel_bias, seq):
    big = 1 << 30
    onehot = jax.nn.one_hot(_t5_bucket(jnp.arange(seq, dtype=jnp.int32)), NUM_BUCKETS, dtype=F32)
    bias_d = jnp.dot(onehot, rel_bias, precision=lax.Precision.HIGHEST).T
    off = 128 * PHASES + 64
    ext = jnp.pad(bias_d, ((0, 0), (off, off)))
    ba, bb, bd = ext[:8], ext[8:16], ext[16:24]
    nt = seq // TILE

    def plain(e, n, hi):
        v = jnp.stack([_generator_rows(e, off, TILE * d, 1) for d in range(n)], axis=1)
        return _toeplitz(v, [(TILE * d, 1, 0, hi, TILE) for d in range(n)])

    t_slc = plain(ba, nt, big)
    t_win = plain(ba, WIN_TILES, WIN_A - 1)
    t_b = plain(bb, 2, WIN_B - 1)
    t_d1 = plain(bd, 2, DIL_MAXDIST)
    v4 = jnp.stack([_generator_rows(bd, off, 4 * dl, 16) for dl in range(-3, 4)], axis=1)
    t_d4 = _toeplitz(v4, [(dl, 4, 0, DIL_MAXDIST, TILE) for dl in range(-3, 4)])
    v16 = jnp.stack([_generator_rows(bd, off, 0, 16)], axis=1)
    t_d16 = _toeplitz(v16, [(0, 1, 0, big, TILE)])
    n_cmp = (seq - CMP_BLK) // CMP_STRIDE + 1
    vc = jnp.stack([_generator_rows(ba, off, p - (CMP_BLK - 1), CMP_STRIDE) for p in range(PHASES)], axis=1)
    t_cmp = _toeplitz(vc, [(p - (CMP_BLK - 1), CMP_STRIDE, 0, big, n_cmp) for p in range(PHASES)],
                      interleave=True)
    return dict(slc=t_slc, win=t_win, b=t_b, d1=t_d1.reshape(4, 2, 2, TILE, TILE),
                d4=t_d4.reshape(4, 2, 7, TILE, TILE), d16=t_d16.reshape(4, 2, TILE, TILE), cmp=t_cmp)


def _const_tables(seq):
    n_cmp = (seq - CMP_BLK) // CMP_STRIDE + 1
    n_slc = seq // SLC_BLK
    c0 = np.arange(TILE)[None, :] * CMP_STRIDE
    s0 = np.arange(n_slc)[:, None] * SLC_BLK
    ovt = ((c0 < s0 + SLC_BLK) & (c0 + CMP_BLK > s0) & (np.arange(TILE)[None, :] < n_cmp))
    row = np.arange(TILE)[None, None, None, :, None]
    lane = np.arange(TILE)[None, None, None, None, :]
    g = np.arange(N_KV)[:, None, None, None, None]
    br = np.arange(3)[None, :, None, None, None]
    cb = np.arange(2)[None, None, :, None, None]
    x_tab = (row == br * N_HEADS + 4 * g + 2 * cb + lane // HEAD_DIM)
    as_bf = lambda m: jnp.asarray(m.astype(np.float32), BF16)
    return as_bf(ovt), as_bf(x_tab)


W_IN_GATES = 1280
PREP_COLS = 2 * LANES


def _prep_w_in_kernel(wt_ref, o_ref):
    w = wt_ref[0]
    q = HEAD_DIM
    swapped = jnp.concatenate([w[0:q], w[2 * q:3 * q], w[q:2 * q], w[3 * q:4 * q]], axis=0)
    w = jnp.where(pl.program_id(1) == CB_ACMP // 2, swapped, w)
    o_ref[...] = w.T.astype(BF16)


def _prep_w_in(w_in):
    depth, d, _ = w_in.shape
    wt = jnp.swapaxes(w_in, 1, 2)
    n_pairs = (N_CB + 1) // 2
    first_shifted = CB_AGATE // 2

    def src(j):
        start = jnp.where(j < first_shifted, PREP_COLS * j,
                          jnp.where(j < n_pairs - 1, PREP_COLS * j + 24, W_IN_GATES))
        return pl.multiple_of(start, 8)

    return pl.pallas_call(
        _prep_w_in_kernel,
        grid=(depth, n_pairs),
        in_specs=[pl.BlockSpec((pl.Element(1), pl.Element(PREP_COLS), pl.Element(d)),
                               lambda l, j: (l, src(j), 0))],
        out_specs=pl.BlockSpec((None, d, PREP_COLS), lambda l, j: (l, 0, j)),
        out_shape=jax.ShapeDtypeStruct((depth, d, n_pairs * PREP_COLS), BF16),
        compiler_params=_cparams("arbitrary", "arbitrary"),
        name="prep_w_in",
    )(wt)


def _compress_weights(cmp_pos, cmp_w1, cmp_w2):
    half = CMP_BLK // 2
    w1 = cmp_w1.reshape(2, CMP_BLK, HEAD_DIM, CMP_HIDDEN)
    z = jnp.zeros((half, HEAD_DIM, CMP_HIDDEN), F32)

    def stack(part):
        top = jnp.concatenate([w1[0, part], z], axis=-1)
        bot = jnp.concatenate([z, w1[1, part]], axis=-1)
        return jnp.concatenate([top, bot], axis=1).reshape(half * LANES, 2 * CMP_HIDDEN).astype(BF16)

    w_lo, w_hi = stack(slice(0, half)), stack(slice(half, CMP_BLK))
    z2 = jnp.zeros((CMP_HIDDEN, HEAD_DIM), F32)
    w2 = jnp.concatenate([jnp.concatenate([cmp_w2[0], z2], axis=1),
                          jnp.concatenate([z2, cmp_w2[1]], axis=1)], axis=0).astype(BF16)
    pos = jnp.concatenate([cmp_pos[0], cmp_pos[1]], axis=-1)
    pos2 = pos.reshape(2, half * LANES)
    return pos2, w_lo, w_hi, w2


def _layer(x2, bsz, seq, layer, norm_w, w_perm, w_out, conv_w, sinks, cmp_pos, cmp_w1, cmp_w2, final_w, final,
           tabs, consts):
    ovt, x_tab = consts
    proj = _inproj(x2, norm_w, w_perm, layer)
    proj4 = proj.reshape(N_CB, bsz, seq, LANES)
    kvcmp = _compress(proj4, *_compress_weights(cmp_pos, cmp_w1, cmp_w2))
    o_cmp, feat = _cmp_select(proj4, kvcmp, tabs["cmp"], ovt)
    mix_a = _slcwin(proj4, o_cmp, feat, tabs["slc"], tabs["win"], x_tab)
    mix_b, mix_c = _swa_conv(proj4, sinks, tabs["b"], conv_w)
    o1, lse1 = _dil1(proj4, tabs["d1"])
    mix_d = _dil(proj4, o1, lse1, tabs["d4"], tabs["d16"])
    flat = lambda t: t.reshape(4, bsz * seq, LANES)
    return _outproj(flat(mix_a), flat(mix_b), flat(mix_c), flat(mix_d), w_out.astype(BF16), x2,
                    final_w, final)


def kernel(x, norm_w, w_in, w_out, conv_w, sinks, cmp_pos, cmp_w1, cmp_w2, rel_bias, final_norm_w):
    bsz, seq, _ = x.shape
    depth = norm_w.shape[0]
    tabs = _bias_tables(rel_bias, seq)
    consts = _const_tables(seq)
    w_perm = _prep_w_in(w_in)
    x2 = x.reshape(bsz * seq, D_MODEL)
    for layer in range(depth):
        x2 = _layer(x2, bsz, seq, layer, norm_w[layer], w_perm, w_out[layer], conv_w[layer], sinks[layer],
                    cmp_pos[layer], cmp_w1[layer], cmp_w2[layer], final_norm_w, layer == depth - 1,
                    tabs, consts)
    return x2.reshape(bsz, seq, D_MODEL)
```

```python
import functools
import math

import numpy as np
import jax
import jax.numpy as jnp
from jax import lax
from jax.experimental import pallas as pl
from jax.experimental.pallas import tpu as pltpu

F32 = jnp.float32
BF16 = jnp.bfloat16

D_MODEL = 2048
HEAD_DIM = 64
N_HEADS = 8
N_KV = 2
CMP_BLK = 32
CMP_STRIDE = 16
CMP_HIDDEN = 128
SLC_BLK = 64
N_SEL = 8
WIN_A = 512
WIN_B = 128
DIL_MAXDIST = 128
NUM_BUCKETS = 32
MAX_DISTANCE = 2048
RMS_EPS = 1e-6
NEG = -1e30
BIG = 1e30
FORCE = 1e4
D_IN = 7192
PHASES = 16
LOG2E = math.log2(math.e)
Q_SCALE = HEAD_DIM ** -0.5 * LOG2E

LANES = 128
TILE = 128
VMEM_LIMIT = 56 * 1024 * 1024

SEL_LANE = (HEAD_DIM, 0)
PAD_LANE = (HEAD_DIM + 32, 32)
ONE_LANE = (HEAD_DIM, 0)

CB_AQ = 0
CB_ACMP = 4
CB_AKS, CB_AVS, CB_AKW, CB_AVW = 6, 7, 8, 9
CB_AGATE = 10
CB_BQ, CB_BK, CB_BV, CB_BGATE = 14, 18, 19, 20
CB_CB, CB_CC, CB_CH, CB_CGATE = 24, 28, 32, 36
CB_DQ, CB_DK, CB_DV, CB_DGATE = 40, 44, 48, 52
CB_AGATES = 56
N_CB = 57
CB_PER_STEP = 19

SLC_CHUNK = 4 * TILE
WIN_TILES = 5
SWA_STREAMS = 8
DIL1_STREAMS = 16
SLC_STREAMS = 4
CMP_ROWS = 16 * TILE
CONV_ROWS = 2 * TILE


def _cparams(*sem):
    return pltpu.CompilerParams(dimension_semantics=sem, vmem_limit_bytes=VMEM_LIMIT)


def _lane():
    return lax.broadcasted_iota(jnp.int32, (1, LANES), 1)


def _lane_lo():
    return _lane() < HEAD_DIM


def _dot_nt(a, b):
    return lax.dot_general(a, b, (((1,), (1,)), ((), ())), preferred_element_type=F32)


def _dot(a, b):
    return jnp.dot(a, b, preferred_element_type=F32)


def _silu(x):
    return x * jax.nn.sigmoid(x)


def _inproj_kernel(x_ref, nw_ref, w_ref, o_ref):
    x = x_ref[...]
    y = x * lax.rsqrt(jnp.mean(x * x, axis=-1, keepdims=True) + RMS_EPS)
    xn = (y * nw_ref[...]).astype(BF16)
    acc = _dot(xn, w_ref[...])
    for k in range(CB_PER_STEP):
        o_ref[k] = acc[:, k * LANES:(k + 1) * LANES]


def _inproj(x2, norm_w, w_perm, layer, tm=512):
    m = x2.shape[0]
    n_steps = N_CB // CB_PER_STEP
    return pl.pallas_call(
        _inproj_kernel,
        grid=(n_steps, m // tm),
        in_specs=[
            pl.BlockSpec((tm, D_MODEL), lambda j, i: (i, 0)),
            pl.BlockSpec((1, D_MODEL), lambda j, i: (0, 0)),
            pl.BlockSpec((None, D_MODEL, CB_PER_STEP * LANES), lambda j, i: (layer, 0, j)),
        ],
        out_specs=pl.BlockSpec((CB_PER_STEP, tm, LANES), lambda j, i: (j, i, 0)),
        out_shape=jax.ShapeDtypeStruct((N_CB, m, LANES), F32),
        compiler_params=_cparams("arbitrary", "arbitrary"),
        name="inproj",
    )(x2, norm_w.reshape(1, D_MODEL), w_perm)


def _outproj_kernel(a_ref, b_ref, c_ref, d_ref, w_ref, x_ref, fw_ref, o_ref, *, final):
    mix = jnp.concatenate([r[k] for r in (a_ref, b_ref, c_ref, d_ref) for k in range(4)], axis=1)
    y = x_ref[...] + _dot(mix.astype(BF16), w_ref[...])
    if final:
        y = y * lax.rsqrt(jnp.mean(y * y, axis=-1, keepdims=True) + RMS_EPS) * fw_ref[...]
    o_ref[...] = y


def _outproj(mix_a, mix_b, mix_c, mix_d, w_out_bf, x2, final_w, final, tm=512):
    m = x2.shape[0]
    mspec = pl.BlockSpec((4, tm, LANES), lambda i: (0, i, 0))
    return pl.pallas_call(
        functools.partial(_outproj_kernel, final=final),
        grid=(m // tm,),
        in_specs=[mspec, mspec, mspec, mspec,
                  pl.BlockSpec((D_MODEL, D_MODEL), lambda i: (0, 0)),
                  pl.BlockSpec((tm, D_MODEL), lambda i: (i, 0)),
                  pl.BlockSpec((1, D_MODEL), lambda i: (0, 0))],
        out_specs=pl.BlockSpec((tm, D_MODEL), lambda i: (i, 0)),
        out_shape=jax.ShapeDtypeStruct((m, D_MODEL), F32),
        compiler_params=_cparams("arbitrary"),
        name="outproj",
    )(mix_a, mix_b, mix_c, mix_d, w_out_bf, x2, final_w.reshape(1, D_MODEL))


def _pad_feat():
    lane = _lane()
    row = jnp.where((lane == PAD_LANE[0]) | (lane == PAD_LANE[1]), -1.0, 0.0)
    return jnp.broadcast_to(row, (TILE, LANES)).astype(BF16)


def _store_kv_aug(k, v, ka, va, pad_rows, with_blocks):
    lo = _lane_lo()
    lane = _lane()
    seq = k.shape[0]
    blk = lax.broadcasted_iota(jnp.int32, (seq, 1), 0) // SLC_BLK
    for h in range(2):
        own = lo if h == 0 else jnp.logical_not(lo)
        kfeat = jnp.where(lane - SEL_LANE[h] == blk, BIG, 0.0) if with_blocks else 0.0
        vfeat = jnp.where(lane == ONE_LANE[h], 1.0, 0.0)
        ka[h][pad_rows:pad_rows + seq, :] = jnp.where(own, k, kfeat).astype(BF16)
        va[h][pad_rows:pad_rows + seq, :] = jnp.where(own, v, vfeat).astype(BF16)
        if pad_rows:
            flag = jnp.where(lane == PAD_LANE[h], BIG, 0.0)
            ka[h][0:pad_rows, :] = jnp.broadcast_to(flag, (pad_rows, LANES)).astype(BF16)
            va[h][0:pad_rows, :] = jnp.zeros((pad_rows, LANES), BF16)


def _group_both_halves(g, x):
    own = jnp.where(_lane_lo() == (g == 0), x, 0.0)
    return own + pltpu.roll(own, HEAD_DIM, axis=1)


def _attend(streams, n_chunks, kc, pad_rows, ka, va, bias_fn, head=False):
    lo = _lane_lo()
    ncb = len(streams[0][0])
    nt = kc // TILE
    qas = [(jnp.concatenate([jnp.where(lo, q, qfeat) for q in q_list], axis=0),
            jnp.concatenate([jnp.where(lo, qfeat, q) for q in q_list], axis=0))
           for q_list, qfeat, _ in streams]

    def chunk_windows(k):
        return [(pl.multiple_of(pad_rows + (i + 1) * TILE - (k + 1) * kc, TILE), nt,
                 lambda jt: k * nt + (nt - 1 - jt)) for _, _, i in streams]

    def step(windows, carry):
        scores = [[_dot_nt(qa[h], ka[h][pl.ds(start, n * TILE), :]) for h in range(2)]
                  for qa, (start, n, _) in zip(qas, windows)]
        probs, new_ms, alphas = [], [], []
        for si, (ms, _) in enumerate(carry):
            _, n, dist = windows[si]
            ms_s = [None] * (2 * ncb)
            p_s, a_s = [], []
            for h in range(2):
                s = scores[si][h]
                ps, al = [], []
                for cb in range(ncb):
                    blks = [s[cb * TILE:(cb + 1) * TILE, jt * TILE:(jt + 1) * TILE]
                            + bias_fn(2 * cb + h, dist(jt)) for jt in range(n)]
                    mx = blks[0]
                    for b in blks[1:]:
                        mx = jnp.maximum(mx, b)
                    m_old = ms[2 * cb + h]
                    m_new = jnp.maximum(m_old, jnp.max(mx, axis=-1, keepdims=True))
                    al.append(jnp.broadcast_to(jnp.exp2(m_old - m_new), (TILE, LANES)))
                    ps.append(jnp.concatenate([jnp.exp2(b - m_new).astype(BF16) for b in blks], axis=1))
                    ms_s[2 * cb + h] = m_new
                p_s.append(jnp.concatenate(ps, axis=0))
                a_s.append(jnp.concatenate(al, axis=0))
            probs.append(p_s)
            alphas.append(a_s)
            new_ms.append(tuple(ms_s))
        out = []
        for si, (_, accs) in enumerate(carry):
            start, n, _ = windows[si]
            new_accs = tuple(accs[h] * alphas[si][h] + _dot(probs[si][h], va[h][pl.ds(start, n * TILE), :])
                             for h in range(2))
            out.append((new_ms[si], new_accs))
        return tuple(out)

    init = tuple((tuple(jnp.full((TILE, 1), NEG, F32) for _ in range(2 * ncb)),
                  tuple(jnp.zeros((ncb * TILE, LANES), F32) for _ in range(2))) for _ in streams)
    if n_chunks is None:
        return step(chunk_windows(0), init)
    if not head:
        return lax.fori_loop(0, n_chunks, lambda k, carry: step(chunk_windows(k), carry), init)

    assert len(streams) == nt
    carry = lax.fori_loop(0, n_chunks - 1, lambda k, carry: step(chunk_windows(k), carry), init)
    head_windows = [(pad_rows, u + 1, lambda jt, i=i: i - jt) for u, (_, _, i) in enumerate(streams)]
    return step(head_windows, carry)


def _split_acc(accs, cb):
    a0 = accs[0][cb * TILE:(cb + 1) * TILE, :]
    a1 = accs[1][cb * TILE:(cb + 1) * TILE, :]
    return a0, a0[:, ONE_LANE[0]:ONE_LANE[0] + 1], a1, a1[:, ONE_LANE[1]:ONE_LANE[1] + 1]


def _swa_conv_kernel(sink_ref, q_ref, k_ref, v_ref, gate_ref, bias_ref, cw_ref, cbb_ref, cc_ref, ch_ref, cg_ref,
                     o_ref, oc_ref, ka0, ka1, va0, va1, pad_ref, *, seq):
    g = pl.program_id(1)
    lo = _lane_lo()
    pad_ref[0:8, :] = jnp.zeros((8, LANES), F32)
    for cb in range(2):
        w = cw_ref[:, cb * LANES:(cb + 1) * LANES]
        for r in range(0, seq, CONV_ROWS):
            pad_ref[8 + r:8 + r + CONV_ROWS, :] = (cc_ref[cb, r:r + CONV_ROWS, :]
                                                   * ch_ref[cb, r:r + CONV_ROWS, :])
        for r in range(0, seq, CONV_ROWS):
            y = (w[0:1, :] * pad_ref[6 + r:6 + r + CONV_ROWS, :]
                 + w[1:2, :] * pad_ref[7 + r:7 + r + CONV_ROWS, :]
                 + w[2:3, :] * pad_ref[8 + r:8 + r + CONV_ROWS, :])
            oc_ref[cb, r:r + CONV_ROWS, :] = (cbb_ref[cb, r:r + CONV_ROWS, :] * y
                                              * _silu(cg_ref[cb, r:r + CONV_ROWS, :]))

    ka, va = (ka0, ka1), (va0, va1)
    _store_kv_aug(_group_both_halves(g, k_ref[...]), _group_both_halves(g, v_ref[...]), ka, va, TILE, False)
    qfeat = _pad_feat()

    def qgroup(it, _):
        tiles = [it * SWA_STREAMS + u for u in range(SWA_STREAMS)]
        rows = [pl.ds(pl.multiple_of(i * TILE, TILE), TILE) for i in tiles]
        streams = [([(q_ref[cb, r, :] * Q_SCALE).astype(BF16) for cb in range(2)], qfeat, i)
                   for i, r in zip(tiles, rows)]
        res = _attend(streams, None, 2 * TILE, TILE, ka, va, lambda h, d: bias_ref[h, d])
        for (ms, accs), r in zip(res, rows):
            for cb in range(2):
                a0, l0, a1, l1 = _split_acc(accs, cb)
                outs = []
                for h, (a, l) in enumerate(((a0, l0), (a1, l1))):
                    m = ms[2 * cb + h]
                    sink = sink_ref[g * 4 + 2 * cb + h] * LOG2E
                    mt = jnp.maximum(m, sink)
                    scale = jnp.exp2(m - mt)
                    outs.append(a * (scale / (l * scale + jnp.exp2(sink - mt))))
                o = jnp.where(lo, outs[0], outs[1])
                o_ref[cb, r, :] = o * _silu(gate_ref[cb, r, :])
        return 0

    lax.fori_loop(0, seq // (TILE * SWA_STREAMS), qgroup, 0)


def _swa_conv(proj4, sinks, t_b, conv_w):
    _, bsz, seq, _ = proj4.shape
    cw = jnp.pad(conv_w, ((0, 5), (0, 0)))

    def two(base):
        return pl.BlockSpec((2, None, seq, LANES), lambda b, g: (base // 2 + g, b, 0, 0))

    def one(cb):
        return pl.BlockSpec((None, None, seq, LANES), lambda b, g: (cb, b, 0, 0))

    ospec = pl.BlockSpec((2, None, seq, LANES), lambda b, g: (g, b, 0, 0))
    oshape = jax.ShapeDtypeStruct((4, bsz, seq, LANES), F32)
    return pl.pallas_call(
        functools.partial(_swa_conv_kernel, seq=seq),
        grid=(bsz, N_KV),
        in_specs=[
            pl.BlockSpec(memory_space=pltpu.SMEM),
            two(CB_BQ), one(CB_BK), one(CB_BV), two(CB_BGATE),
            pl.BlockSpec((4, 2, TILE, TILE), lambda b, g: (g, 0, 0, 0)),
            pl.BlockSpec((8, 2 * LANES), lambda b, g: (0, g)),
            two(CB_CB), two(CB_CC), two(CB_CH), two(CB_CGATE),
        ],
        out_specs=(ospec, ospec),
        out_shape=(oshape, oshape),
        scratch_shapes=[pltpu.VMEM((seq + TILE, LANES), BF16)] * 4 + [pltpu.VMEM((seq + 8, LANES), F32)],
        compiler_params=_cparams("arbitrary", "arbitrary"),
        name="swa_conv",
    )(sinks, proj4, proj4, proj4, proj4, t_b, cw, proj4, proj4, proj4, proj4)


def _dil1_kernel(q_ref, k_ref, v_ref, bias_ref, o_ref, lse_ref, ka0, ka1, va0, va1, *, seq):
    lo = _lane_lo()
    ka, va = (ka0, ka1), (va0, va1)
    _store_kv_aug(k_ref[...], v_ref[...], ka, va, TILE, False)
    qfeat = _pad_feat()

    def qgroup(it, _):
        tiles = [it * DIL1_STREAMS + u for u in range(DIL1_STREAMS)]
        rows = [pl.ds(pl.multiple_of(i * TILE, TILE), TILE) for i in tiles]
        streams = [([(q_ref[r, :] * Q_SCALE).astype(BF16)], qfeat, i) for i, r in zip(tiles, rows)]
        res = _attend(streams, None, 2 * TILE, TILE, ka, va, lambda h, d: bias_ref[h, d])
        for (ms, accs), r in zip(res, rows):
            a0, l0, a1, l1 = _split_acc(accs, 0)
            o_ref[r, :] = jnp.where(lo, a0 / l0, a1 / l1)
            lse_ref[r, :] = jnp.where(lo, ms[0] + jnp.log2(l0), ms[1] + jnp.log2(l1))
        return 0

    lax.fori_loop(0, seq // (TILE * DIL1_STREAMS), qgroup, 0)


def _dil1(proj4, t_d1):
    _, bsz, seq, _ = proj4.shape

    def spec(base):
        return pl.BlockSpec((None, None, seq, LANES), lambda b, h: (base + h, b, 0, 0))

    ospec = pl.BlockSpec((None, None, seq, LANES), lambda b, h: (h, b, 0, 0))
    oshape = jax.ShapeDtypeStruct((4, bsz, seq, LANES), F32)
    return pl.pallas_call(
        functools.partial(_dil1_kernel, seq=seq),
        grid=(bsz, 4),
        in_specs=[spec(CB_DQ), spec(CB_DK), spec(CB_DV),
                  pl.BlockSpec((None, 2, 2, TILE, TILE), lambda b, h: (h, 0, 0, 0, 0))],
        out_specs=(ospec, ospec),
        out_shape=(oshape, oshape),
        scratch_shapes=[pltpu.VMEM((seq + TILE, LANES), BF16)] * 4,
        compiler_params=_cparams("arbitrary", "arbitrary"),
        name="dil1",
    )(proj4, proj4, proj4, t_d1)


def _dil_kernel(q_ref, k_ref, v_ref, o1_ref, lse1_ref, gate_ref, t4_ref, t16_ref, o_ref,
                qs, kp0, kp1, vp0, vp1, o1s, l1s, gs, res, *, seq):
    lo = _lane_lo()
    na = seq // PHASES
    kp, vp = (kp0, kp1), (vp0, vp1)

    def by_phase(ref):
        return pltpu.einshape("apl->pal", ref[...].reshape(na, PHASES, LANES))

    qs[...] = (by_phase(q_ref) * Q_SCALE).astype(BF16)
    k = by_phase(k_ref)
    kp0[...] = jnp.where(lo, k, 0.0).astype(BF16)
    kp1[...] = jnp.where(lo, 0.0, k).astype(BF16)
    v = by_phase(v_ref)
    vp0[...] = jnp.where(lo, v, 0.0).astype(BF16)
    vp1[...] = jnp.where(lo, 0.0, v).astype(BF16)
    o1s[...] = by_phase(o1_ref)
    l1s[...] = by_phase(lse1_ref)
    gs[...] = by_phase(gate_ref)

    sub = DIL_MAXDIST // 4
    n_sub = na // sub

    def residue(r, _):
        phases = [4 * c + r for c in range(4)]
        key0 = [max(u * sub - sub, 0) for u in range(n_sub)]
        qsub = [jnp.concatenate([qs[p, u * sub:(u + 1) * sub, :] for p in phases], axis=0) for u in range(n_sub)]
        sc4 = [[_dot_nt(qsub[u], jnp.concatenate([kp[h][p, key0[u]:key0[u] + 2 * sub, :] for p in phases], axis=0))
                for u in range(n_sub)] for h in range(2)]
        sc16 = [[_dot_nt(qs[p], kp[h][p]) for p in phases] for h in range(2)]
        p4, l4, m4, p16, l16, m16 = [], [], [], [], [], []
        for h in range(2):
            for u in range(n_sub):
                x = sc4[h][u] + t4_ref[h, min(u, 1)]
                m = jnp.max(x, axis=-1, keepdims=True)
                e = jnp.exp2(x - m)
                l4.append(jnp.sum(e, axis=-1, keepdims=True))
                m4.append(m)
                p4.append(e.astype(BF16))
            for c in range(4):
                x = sc16[h][c] + t16_ref[h]
                m = jnp.max(x, axis=-1, keepdims=True)
                e = jnp.exp2(x - m)
                l16.append(jnp.sum(e, axis=-1, keepdims=True))
                m16.append(m)
                p16.append(e.astype(BF16))
        pv4 = [[_dot(p4[n_sub * h + u],
                     jnp.concatenate([vp[h][p, key0[u]:key0[u] + 2 * sub, :] for p in phases], axis=0))
                for u in range(n_sub)] for h in range(2)]
        pv16 = [[_dot(p16[4 * h + c], vp[h][phases[c]]) for c in range(4)] for h in range(2)]

        def phase_rows(pieces, c):
            return jnp.concatenate([x[c * sub:(c + 1) * sub] for x in pieces], axis=0)

        for c, p in enumerate(phases):
            o4 = sum(phase_rows([pv4[h][u] / l4[n_sub * h + u] for u in range(n_sub)], c) for h in range(2))
            lse4 = [phase_rows([m4[n_sub * h + u] + jnp.log2(l4[n_sub * h + u]) for u in range(n_sub)], c)
                    for h in range(2)]
            o16 = pv16[0][c] / l16[c] + pv16[1][c] / l16[4 + c]
            lb = jnp.where(lo, lse4[0], lse4[1])
            lc = jnp.where(lo, m16[c] + jnp.log2(l16[c]), m16[4 + c] + jnp.log2(l16[4 + c]))
            la = l1s[p]
            mx = jnp.maximum(jnp.maximum(la, lb), lc)
            ea, eb, ec = jnp.exp2(la - mx), jnp.exp2(lb - mx), jnp.exp2(lc - mx)
            o = (ea * o1s[p] + eb * o4 + ec * o16) / (ea + eb + ec)
            res[p] = o * _silu(gs[p])
        return 0

    lax.fori_loop(0, 4, residue, 0)
    o_ref[...] = pltpu.einshape("pal->apl", res[...]).reshape(seq, LANES)


def _dil(proj4, o1, lse1, t_d4, t_d16):
    _, bsz, seq, _ = proj4.shape
    na = seq // PHASES

    def spec(base):
        return pl.BlockSpec((None, None, seq, LANES), lambda b, h: (base + h, b, 0, 0))

    return pl.pallas_call(
        functools.partial(_dil_kernel, seq=seq),
        grid=(bsz, 4),
        in_specs=[spec(CB_DQ), spec(CB_DK), spec(CB_DV), spec(0), spec(0), spec(CB_DGATE),
                  pl.BlockSpec((None, 2, 2, TILE, 2 * TILE), lambda b, h: (h, 0, 0, 0, 0)),
                  pl.BlockSpec((None, 2, TILE, TILE), lambda b, h: (h, 0, 0, 0))],
        out_specs=spec(0),
        out_shape=jax.ShapeDtypeStruct((4, bsz, seq, LANES), F32),
        scratch_shapes=[pltpu.VMEM((PHASES, na, LANES), BF16)] * 5 + [pltpu.VMEM((PHASES, na, LANES), F32)] * 4,
        compiler_params=_cparams("arbitrary", "arbitrary"),
        name="dil",
    )(proj4, proj4, proj4, o1, lse1, proj4, t_d4, t_d16)


def _compress_kernel(kv_ref, pos_ref, wlo_ref, whi_ref, w2_ref, o_ref, *, seq):
    na = seq // PHASES
    ph = pltpu.einshape("apl->pal", kv_ref[...].reshape(na, PHASES, LANES))
    r = jnp.concatenate([ph[p] for p in range(PHASES)], axis=1)
    h_lo = _dot((r + pos_ref[0:1, :]).astype(BF16), wlo_ref[...])
    h_hi = _dot((r + pos_ref[1:2, :]).astype(BF16), whi_ref[...])
    h = h_lo + pltpu.roll(h_hi, na - 1, axis=0)
    o_ref[...] = _dot(_silu(h).astype(BF16), w2_ref[...])


def _compress(proj4, pos2, w_lo, w_hi, w2):
    _, bsz, seq, _ = proj4.shape
    na = seq // PHASES
    wide = PHASES * LANES
    full = lambda shape: pl.BlockSpec(shape, lambda b, g: (0,) * len(shape))
    return pl.pallas_call(
        functools.partial(_compress_kernel, seq=seq),
        grid=(bsz, N_KV),
        in_specs=[pl.BlockSpec((None, None, seq, LANES), lambda b, g: (CB_ACMP + g, b, 0, 0)),
                  full((2, wide)), full((wide, 2 * CMP_HIDDEN)), full((wide, 2 * CMP_HIDDEN)),
                  full((2 * CMP_HIDDEN, LANES))],
        out_specs=pl.BlockSpec((None, None, na, LANES), lambda b, g: (b, g, 0, 0)),
        out_shape=jax.ShapeDtypeStruct((bsz, N_KV, na, LANES), F32),
        compiler_params=_cparams("arbitrary", "arbitrary"),
        name="compress",
    )(proj4, pos2, w_lo, w_hi, w2)


def _cmp_kernel(q_ref, kv_ref, bias_ref, ovt_ref, o_ref, feat_ref, *, seq):
    lo = _lane_lo()
    kv = kv_ref[...]
    k_lo = jnp.where(lo, kv, 0.0)
    v_hi = jnp.where(lo, 0.0, kv)
    kp = (k_lo.astype(BF16), pltpu.roll(k_lo, HEAD_DIM, axis=1).astype(BF16))
    vp = (pltpu.roll(v_hi, HEAD_DIM, axis=1).astype(BF16), v_hi.astype(BF16))
    ovt = ovt_ref[...]
    n_slc = ovt.shape[0]
    first = lax.broadcasted_iota(jnp.int32, (n_slc, CMP_ROWS), 0) == 0
    pad_rows = jnp.where(first, -1.0, 0.0).astype(BF16)
    eye = (lax.broadcasted_iota(jnp.int32, (TILE, TILE), 0)
           == lax.broadcasted_iota(jnp.int32, (TILE, TILE), 1)).astype(BF16)

    def qgroup(it, _):
        r0 = pl.multiple_of(it * CMP_ROWS, CMP_ROWS)
        rows = pl.ds(r0, CMP_ROWS)
        qs = [(q_ref[cb, rows, :] * Q_SCALE).astype(BF16) for cb in range(2)]
        scores = [[_dot_nt(qs[cb], kp[hh]) for hh in range(2)] for cb in range(2)]
        probs = [[None, None], [None, None]]
        psum = jnp.zeros((CMP_ROWS, LANES), F32)
        for cb in range(2):
            for hh in range(2):
                s = scores[cb][hh] + bias_ref[2 * cb + hh, rows, :]
                m = jnp.maximum(jnp.max(s, axis=-1, keepdims=True), -1e20)
                e = jnp.exp2(s - m)
                p = e / jnp.maximum(jnp.sum(e, axis=-1, keepdims=True), 1e-30)
                psum = psum + p
                probs[cb][hh] = p.astype(BF16)
        for cb in range(2):
            o_ref[cb, rows, :] = _dot(probs[cb][0], vp[0]) + _dot(probs[cb][1], vp[1])
        p1 = psum.astype(BF16)
        rem = psum - p1.astype(F32)
        p2 = rem.astype(BF16)
        p3 = (rem - p2.astype(F32)).astype(BF16)
        imp = _dot_nt(ovt, p1) + _dot_nt(ovt, p2) + _dot_nt(ovt, p3)
        blk = lax.broadcasted_iota(jnp.int32, (n_slc, CMP_ROWS), 0)
        t = r0 + lax.broadcasted_iota(jnp.int32, (n_slc, CMP_ROWS), 1)
        cur = t // SLC_BLK
        forced = (blk == 0) | (blk == cur) | (blk == cur - 1)
        imp = jnp.where(blk > cur, NEG, jnp.where(forced, FORCE, imp))
        rank = jnp.zeros((n_slc, CMP_ROWS), F32)
        for mth in range(n_slc):
            row = imp[mth:mth + 1, :]
            tie = (blk > mth).astype(F32)
            rank = rank + jnp.where(row > imp, 1.0, jnp.where(row == imp, tie, 0.0))
        unsel = jnp.where(rank < N_SEL, 0.0, -1.0).astype(BF16)
        half = jnp.concatenate([unsel, pad_rows], axis=0)
        feat_t = jnp.concatenate([half, half], axis=0)
        for u in range(CMP_ROWS // TILE):
            piece = _dot_nt(eye, feat_t[:, u * TILE:(u + 1) * TILE])
            feat_ref[pl.ds(r0 + u * TILE, TILE), :] = piece.astype(BF16)
        return 0

    lax.fori_loop(0, seq // CMP_ROWS, qgroup, 0)


def _cmp_select(proj4, kvcmp, t_cmp, ovt):
    _, bsz, seq, _ = proj4.shape
    return pl.pallas_call(
        functools.partial(_cmp_kernel, seq=seq),
        grid=(bsz, N_KV),
        in_specs=[pl.BlockSpec((2, None, seq, LANES), lambda b, g: (CB_AQ // 2 + g, b, 0, 0)),
                  pl.BlockSpec((None, None, TILE, LANES), lambda b, g: (b, g, 0, 0)),
                  pl.BlockSpec((4, seq, LANES), lambda b, g: (g, 0, 0)),
                  pl.BlockSpec(ovt.shape, lambda b, g: (0, 0))],
        out_specs=(pl.BlockSpec((2, None, seq, LANES), lambda b, g: (g, b, 0, 0)),
                   pl.BlockSpec((None, None, seq, LANES), lambda b, g: (b, g, 0, 0))),
        out_shape=(jax.ShapeDtypeStruct((4, bsz, seq, LANES), F32),
                   jax.ShapeDtypeStruct((bsz, N_KV, seq, LANES), BF16)),
        compiler_params=_cparams("arbitrary", "arbitrary"),
        name="cmp_select",
    )(proj4, kvcmp, t_cmp, ovt)


def _slcwin_kernel(q_ref, ks_ref, vs_ref, kw_ref, vw_ref, ocmp_ref, feat_ref, gates_ref, agate_ref,
                   tslc_ref, twin_ref, x_ref, o_ref,
                   ksa0, ksa1, vsa0, vsa1, kwa0, kwa1, vwa0, vwa1, *, seq):
    g = pl.program_id(1)
    lo = _lane_lo()
    ksa, vsa, kwa, vwa = (ksa0, ksa1), (vsa0, vsa1), (kwa0, kwa1), (vwa0, vwa1)
    win_pad = (WIN_TILES - 1) * TILE
    _store_kv_aug(_group_both_halves(g, ks_ref[...]), _group_both_halves(g, vs_ref[...]),
                  ksa, vsa, 0, True)
    _store_kv_aug(_group_both_halves(g, kw_ref[...]), _group_both_halves(g, vw_ref[...]),
                  kwa, vwa, win_pad, False)
    slc_tiles = SLC_CHUNK // TILE

    def qgroup(it, _):
        tiles = [it * SLC_STREAMS + u for u in range(SLC_STREAMS)]
        rows = [pl.ds(pl.multiple_of(i * TILE, TILE), TILE) for i in tiles]
        streams = [([(q_ref[cb, r, :] * Q_SCALE).astype(BF16) for cb in range(2)],
                    feat_ref[r, :], i) for i, r in zip(tiles, rows)]
        n_chunks = (tiles[0] + slc_tiles) // slc_tiles
        res_w = _attend(streams, None, WIN_TILES * TILE, win_pad, kwa, vwa, lambda h, d: twin_ref[h, d])
        res_s = _attend(streams, n_chunks, SLC_CHUNK, 0, ksa, vsa, lambda h, d: tslc_ref[h, d], head=True)
        for (_, acc_s), (_, acc_w), r in zip(res_s, res_w, rows):
            sg = jax.nn.sigmoid(gates_ref[r, :])
            sg1 = sg.astype(BF16)
            sg2 = (sg - sg1.astype(F32)).astype(BF16)
            for cb in range(2):
                a0, l0, a1, l1 = _split_acc(acc_s, cb)
                o_slc = jnp.where(lo, a0 / l0, a1 / l1)
                a0, l0, a1, l1 = _split_acc(acc_w, cb)
                o_win = jnp.where(lo, a0 / l0, a1 / l1)
                gts = [_dot(sg1, x_ref[br, cb]) + _dot(sg2, x_ref[br, cb]) for br in range(3)]
                o = gts[0] * ocmp_ref[cb, r, :] + gts[1] * o_slc + gts[2] * o_win
                o_ref[cb, r, :] = o * _silu(agate_ref[cb, r, :])
        return 0

    assert slc_tiles % SLC_STREAMS == 0
    lax.fori_loop(0, seq // (TILE * SLC_STREAMS), qgroup, 0)


def _slcwin(proj4, o_cmp, feat, t_slc, t_win, x_tab):
    _, bsz, seq, _ = proj4.shape
    nt = seq // TILE

    def one(cb):
        return pl.BlockSpec((None, None, seq, LANES), lambda b, g: (cb, b, 0, 0))

    def two(base):
        return pl.BlockSpec((2, None, seq, LANES), lambda b, g: (base // 2 + g, b, 0, 0))

    slc_rows = seq
    win_rows = seq + (WIN_TILES - 1) * TILE
    return pl.pallas_call(
        functools.partial(_slcwin_kernel, seq=seq),
        grid=(bsz, N_KV),
        in_specs=[two(CB_AQ), one(CB_AKS), one(CB_AVS), one(CB_AKW), one(CB_AVW),
                  pl.BlockSpec((2, None, seq, LANES), lambda b, g: (g, b, 0, 0)),
                  pl.BlockSpec((None, None, seq, LANES), lambda b, g: (b, g, 0, 0)),
                  one(CB_AGATES), two(CB_AGATE),
                  pl.BlockSpec((4, nt, TILE, TILE), lambda b, g: (g, 0, 0, 0)),
                  pl.BlockSpec((4, WIN_TILES, TILE, TILE), lambda b, g: (g, 0, 0, 0)),
                  pl.BlockSpec((None, 3, 2, TILE, TILE), lambda b, g: (g, 0, 0, 0, 0))],
        out_specs=pl.BlockSpec((2, None, seq, LANES), lambda b, g: (g, b, 0, 0)),
        out_shape=jax.ShapeDtypeStruct((4, bsz, seq, LANES), F32),
        scratch_shapes=[pltpu.VMEM((slc_rows, LANES), BF16)] * 4 + [pltpu.VMEM((win_rows, LANES), BF16)] * 4,
        compiler_params=_cparams("arbitrary", "arbitrary"),
        name="slcwin",
    )(proj4, proj4, proj4, proj4, proj4, o_cmp, feat, proj4, proj4, t_slc, t_win, x_tab)


def _toeplitz_kernel(par_ref, v_ref, o_ref, *, n_tiles, interleave):
    a = lax.broadcasted_iota(jnp.int32, (TILE, TILE), 0)
    b = lax.broadcasted_iota(jnp.int32, (TILE, TILE), 1)
    for t in range(n_tiles):
        x = jnp.broadcast_to(v_ref[t:t + 1, :], (TILE, 2 * TILE))
        y = pltpu.roll(x, 0, 1, stride=1, stride_axis=0)[:, :TILE]
        dist = par_ref[t, 0] + par_ref[t, 1] * (a - b)
        ok = (dist >= par_ref[t, 2]) & (dist <= par_ref[t, 3]) & (b < par_ref[t, 4])
        tile = jnp.where(ok, y * LOG2E, NEG)
        if interleave:
            o_ref[pl.ds(t, TILE, stride=n_tiles), :] = tile
        else:
            o_ref[t] = tile


def _toeplitz(v, params, interleave=False):
    heads, n_tiles, _ = v.shape
    if interleave:
        oshape, ospec = (heads, n_tiles * TILE, TILE), pl.BlockSpec((None, n_tiles * TILE, TILE), lambda h: (h, 0, 0))
    else:
        oshape, ospec = (heads, n_tiles, TILE, TILE), pl.BlockSpec((None, n_tiles, TILE, TILE), lambda h: (h, 0, 0, 0))
    return pl.pallas_call(
        functools.partial(_toeplitz_kernel, n_tiles=n_tiles, interleave=interleave),
        grid=(heads,),
        in_specs=[pl.BlockSpec(memory_space=pltpu.SMEM),
                  pl.BlockSpec((None, n_tiles, 2 * TILE), lambda h: (h, 0, 0))],
        out_specs=ospec,
        out_shape=jax.ShapeDtypeStruct(oshape, F32),
        compiler_params=_cparams("arbitrary"),
        name="toeplitz",
    )(jnp.asarray(params, jnp.int32), v)


def _t5_bucket(d):
    exact = NUM_BUCKETS // 2
    large = exact + (jnp.log(jnp.maximum(d, exact).astype(F32) / exact)
                     / math.log(MAX_DISTANCE / exact) * (NUM_BUCKETS - exact)).astype(jnp.int32)
    return jnp.where(d < exact, d, jnp.minimum(large, NUM_BUCKETS - 1))


def _generator_rows(ext, off, base, mul):
    def run(start):
        if mul == 1:
            seg = ext[:, start:start + TILE]
        else:
            by_phase = ext.reshape(ext.shape[0], -1, mul).transpose(0, 2, 1)
            seg = by_phase[:, start % mul, start // mul:start // mul + TILE]
        return seg[:, ::-1]

    return jnp.concatenate([run(off + base - 127 * mul),
                            run(off + base + mul)], axis=1)


def _bias_tables(rel_bias, seq):
    big = 1 << 30
    onehot = jax.nn.one_hot(_t5_bucket(jnp.arange(seq, dtype=jnp.int32)), NUM_BUCKETS, dtype=F32)
    bias_d = jnp.dot(onehot, rel_bias, precision=lax.Precision.HIGHEST).T
    off = 128 * PHASES + 64
    ext = jnp.pad(bias_d, ((0, 0), (off, off)))
    ba, bb, bd = ext[:8], ext[8:16], ext[16:24]
    nt = seq // TILE

    def plain(e, n, hi):
        v = jnp.stack([_generator_rows(e, off, TILE * d, 1) for d in range(n)], axis=1)
        return _toeplitz(v, [(TILE * d, 1, 0, hi, TILE) for d in range(n)])

    t_slc = plain(ba, nt, big)
    t_win = plain(ba, WIN_TILES, WIN_A - 1)
    t_b = plain(bb, 2, WIN_B - 1)
    t_d1 = plain(bd, 2, DIL_MAXDIST)
    v4 = jnp.stack([_generator_rows(bd, off, 4 * dl, 16) for dl in range(-3, 4)], axis=1)
    t_d4 = _toeplitz(v4, [(dl, 4, 0, DIL_MAXDIST, TILE) for dl in range(-3, 4)])
    sub = DIL_MAXDIST // 4
    t_d4 = jnp.stack([jnp.concatenate([jnp.concatenate([t_d4[:, c - c2 + 3, r0:r0 + sub, 0:2 * sub]
                                                        for c2 in range(4)], axis=-1)
                                       for c in range(4)], axis=-2) for r0 in (0, sub)], axis=1)
    v16 = jnp.stack([_generator_rows(bd, off, 0, 16)], axis=1)
    t_d16 = _toeplitz(v16, [(0, 1, 0, big, TILE)])
    n_cmp = (seq - CMP_BLK) // CMP_STRIDE + 1
    vc = jnp.stack([_generator_rows(ba, off, p - (CMP_BLK - 1), CMP_STRIDE) for p in range(PHASES)], axis=1)
    t_cmp = _toeplitz(vc, [(p - (CMP_BLK - 1), CMP_STRIDE, 0, big, n_cmp) for p in range(PHASES)],
                      interleave=True)
    return dict(slc=t_slc, win=t_win, b=t_b, d1=t_d1.reshape(4, 2, 2, TILE, TILE),
                d4=t_d4.reshape(4, 2, 2, TILE, 2 * TILE), d16=t_d16.reshape(4, 2, TILE, TILE), cmp=t_cmp)


def _const_tables(seq):
    n_cmp = (seq - CMP_BLK) // CMP_STRIDE + 1
    n_slc = seq // SLC_BLK
    c0 = np.arange(TILE)[None, :] * CMP_STRIDE
    s0 = np.arange(n_slc)[:, None] * SLC_BLK
    ovt = ((c0 < s0 + SLC_BLK) & (c0 + CMP_BLK > s0) & (np.arange(TILE)[None, :] < n_cmp))
    row = np.arange(TILE)[None, None, None, :, None]
    lane = np.arange(TILE)[None, None, None, None, :]
    g = np.arange(N_KV)[:, None, None, None, None]
    br = np.arange(3)[None, :, None, None, None]
    cb = np.arange(2)[None, None, :, None, None]
    x_tab = (row == br * N_HEADS + 4 * g + 2 * cb + lane // HEAD_DIM)
    as_bf = lambda m: jnp.asarray(m.astype(np.float32), BF16)
    return as_bf(ovt), as_bf(x_tab)


W_IN_GATES = 1280
PREP_COLS = 2 * LANES


def _prep_w_in_kernel(wt_ref, o_ref):
    w = wt_ref[0]
    q = HEAD_DIM
    swapped = jnp.concatenate([w[0:q], w[2 * q:3 * q], w[q:2 * q], w[3 * q:4 * q]], axis=0)
    w = jnp.where(pl.program_id(1) == CB_ACMP // 2, swapped, w)
    o_ref[...] = w.T.astype(BF16)


def _prep_w_in(w_in):
    depth, d, _ = w_in.shape
    wt = jnp.swapaxes(w_in, 1, 2)
    n_pairs = (N_CB + 1) // 2
    first_shifted = CB_AGATE // 2

    def src(j):
        start = jnp.where(j < first_shifted, PREP_COLS * j,
                          jnp.where(j < n_pairs - 1, PREP_COLS * j + 24, W_IN_GATES))
        return pl.multiple_of(start, 8)

    return pl.pallas_call(
        _prep_w_in_kernel,
        grid=(depth, n_pairs),
        in_specs=[pl.BlockSpec((pl.Element(1), pl.Element(PREP_COLS), pl.Element(d)),
                               lambda l, j: (l, src(j), 0))],
        out_specs=pl.BlockSpec((None, d, PREP_COLS), lambda l, j: (l, 0, j)),
        out_shape=jax.ShapeDtypeStruct((depth, d, n_pairs * PREP_COLS), BF16),
        compiler_params=_cparams("arbitrary", "arbitrary"),
        name="prep_w_in",
    )(wt)


def _compress_weights(cmp_pos, cmp_w1, cmp_w2):
    half = CMP_BLK // 2
    w1 = cmp_w1.reshape(2, CMP_BLK, HEAD_DIM, CMP_HIDDEN)
    z = jnp.zeros((half, HEAD_DIM, CMP_HIDDEN), F32)

    def stack(part):
        top = jnp.concatenate([w1[0, part], z], axis=-1)
        bot = jnp.concatenate([z, w1[1, part]], axis=-1)
        return jnp.concatenate([top, bot], axis=1).reshape(half * LANES, 2 * CMP_HIDDEN).astype(BF16)

    w_lo, w_hi = stack(slice(0, half)), stack(slice(half, CMP_BLK))
    z2 = jnp.zeros((CMP_HIDDEN, HEAD_DIM), F32)
    w2 = jnp.concatenate([jnp.concatenate([cmp_w2[0], z2], axis=1),
                          jnp.concatenate([z2, cmp_w2[1]], axis=1)], axis=0).astype(BF16)
    pos = jnp.concatenate([cmp_pos[0], cmp_pos[1]], axis=-1)
    pos2 = pos.reshape(2, half * LANES)
    return pos2, w_lo, w_hi, w2


def _layer(x2, bsz, seq, layer, norm_w, w_perm, w_out, conv_w, sinks, cmp_pos, cmp_w1, cmp_w2, final_w, final,
           tabs, consts):
    ovt, x_tab = consts
    proj = _inproj(x2, norm_w, w_perm, layer)
    proj4 = proj.reshape(N_CB, bsz, seq, LANES)
    kvcmp = _compress(proj4, *_compress_weights(cmp_pos, cmp_w1, cmp_w2))
    o_cmp, feat = _cmp_select(proj4, kvcmp, tabs["cmp"], ovt)
    mix_a = _slcwin(proj4, o_cmp, feat, tabs["slc"], tabs["win"], x_tab)
    mix_b, mix_c = _swa_conv(proj4, sinks, tabs["b"], conv_w)
    o1, lse1 = _dil1(proj4, tabs["d1"])
    mix_d = _dil(proj4, o1, lse1, tabs["d4"], tabs["d16"])
    flat = lambda t: t.reshape(4, bsz * seq, LANES)
    return _outproj(flat(mix_a), flat(mix_b), flat(mix_c), flat(mix_d), w_out.astype(BF16), x2,
                    final_w, final)


def kernel(x, norm_w, w_in, w_out, conv_w, sinks, cmp_pos, cmp_w1, cmp_w2, rel_bias, final_norm_w):
    bsz, seq, _ = x.shape
    depth = norm_w.shape[0]
    tabs = _bias_tables(rel_bias, seq)
    consts = _const_tables(seq)
    w_perm = _prep_w_in(w_in)
    x2 = x.reshape(bsz * seq, D_MODEL)
    for layer in range(depth):
        x2 = _layer(x2, bsz, seq, layer, norm_w[layer], w_perm, w_out[layer], conv_w[layer], sinks[layer],
                    cmp_pos[layer], cmp_w1[layer], cmp_w2[layer], final_norm_w, layer == depth - 1,
                    tabs, consts)
    return x2.reshape(bsz, seq, D_MODEL)
```
